```python
import jax, jax.numpy as jnp
from jax import lax
import numpy as np

D_MODEL = 2048
BATCH = 32
SEQ = 256
DEPTH = 2
DEC_BATCH = 4
DEC_SEQ = 4096
PAST_LEN = 512

GRID_W = 64
D_MIX = D_MODEL
D_A = D_MIX // 4
G_A = 4
CONV_K = 31
D_B = D_MIX // 4
G_B = 4
CHUNK = 128
NOPE_DIM = 128
ROPE_DIM = 64
V_DIM = 128
H_C = (D_MIX - D_A - D_B) // V_DIM
Q_RANK = D_MODEL // 4
KV_RANK = D_MODEL // 8
D_FF = 11 * D_MODEL // 4
FFN_K = 3
Q_BLOCK = 128
ROPE_THETA = 10000.0
EPS = 1e-6
N_IN = 2 * D_A + 2 * D_B + Q_RANK + KV_RANK + ROPE_DIM
SPLITS = [2 * D_A, 2 * D_A + 2 * D_B, 2 * D_A + 2 * D_B + Q_RANK, 2 * D_A + 2 * D_B + Q_RANK + KV_RANK]
ATTN_SCALE = (NOPE_DIM + ROPE_DIM) ** -0.5

kernel_name = 'hybrid_conv_gmlp_mla_diffusion_step'


def rmsnorm(x, g):
    xf = x.astype(jnp.float32)
    y = xf * lax.rsqrt(jnp.mean(xf * xf, axis=-1, keepdims=True) + EPS)
    return (y * g.astype(jnp.float32)).astype(x.dtype)


def group_layernorm(x, g, b, groups):
    xf = x.astype(jnp.float32).reshape(x.shape[:-1] + (groups, x.shape[-1] // groups))
    mu = jnp.mean(xf, axis=-1, keepdims=True)
    xc = xf - mu
    var = jnp.mean(xc * xc, axis=-1, keepdims=True)
    y = (xc * lax.rsqrt(var + EPS)).reshape(x.shape)
    return (y * g.astype(jnp.float32) + b.astype(jnp.float32)).astype(x.dtype)


def dwconv(x, w, b):
    k = w.shape[0]
    y = lax.conv_general_dilated(x, w[:, None, :].astype(x.dtype), window_strides=(1,),
                                 padding=[(k // 2, k // 2)],
                                 dimension_numbers=('NWC', 'WIO', 'NWC'),
                                 feature_group_count=x.shape[-1])
    return y + b


def rope_tables(length):
    rows = length // GRID_W
    r = jnp.repeat(jnp.arange(rows, dtype=jnp.float32), GRID_W)
    col = (jnp.arange(length) % GRID_W).astype(jnp.float32)
    half = ROPE_DIM // 4
    freq = 1.0 / (ROPE_THETA ** (jnp.arange(half, dtype=jnp.float32) / half))
    ang = jnp.stack([r[:, None] * freq, col[:, None] * freq], axis=1)
    return jnp.cos(ang)[:, None], jnp.sin(ang)[:, None]


def apply_rope_2d(x, cos, sin):
    xs = x.astype(jnp.float32).reshape(x.shape[:-1] + (2, 2, ROPE_DIM // 4))
    x1 = xs[..., 0, :]
    x2 = xs[..., 1, :]
    out = jnp.stack([x1 * cos - x2 * sin, x2 * cos + x1 * sin], axis=-2)
    return out.reshape(x.shape).astype(x.dtype)


def attention(q, k, v):
    bsz, sq, h, dk = q.shape
    nb = sq // Q_BLOCK
    qb = q.reshape(bsz, nb, Q_BLOCK, h, dk).transpose(1, 0, 2, 3, 4)

    def one(qi):
        s = jnp.einsum('bqhd,bkhd->bhqk', qi, k).astype(jnp.float32) * ATTN_SCALE
        p = jax.nn.softmax(s, axis=-1).astype(v.dtype)
        return jnp.einsum('bhqk,bkhd->bqhd', p, v)

    o = lax.map(one, qb)
    return o.transpose(1, 0, 2, 3, 4).reshape(bsz, sq, h, v.shape[-1])


def mla_keys_values(ckv_n, kr, w_ukv):
    bsz, length, _ = ckv_n.shape
    kv = (ckv_n @ w_ukv).reshape(bsz, length, H_C, NOPE_DIM + V_DIM)
    k_nope, v = kv[..., :NOPE_DIM], kv[..., NOPE_DIM:]
    k = jnp.concatenate([k_nope, jnp.broadcast_to(kr, (bsz, length, H_C, ROPE_DIM))], axis=-1)
    return k, v


def conv_module(z, conv_w, conv_b, ln_g, ln_b):
    a = z[..., :D_A] * jax.nn.sigmoid(z[..., D_A:])
    a = dwconv(a, conv_w, conv_b)
    a = group_layernorm(a, ln_g, ln_b, G_A)
    return jax.nn.silu(a)


def chunk_spatial_gate(z, ln_g, ln_b, w_s, b_s):
    z = jax.nn.gelu(z, approximate=False)
    u, v = z[..., :D_B], z[..., D_B:]
    v = group_layernorm(v, ln_g, ln_b, G_B)
    bsz, length, _ = v.shape
    vc = v.reshape(bsz, length // CHUNK, CHUNK, G_B, D_B // G_B)
    sv = jnp.einsum('gpq,bcqgd->bcpgd', w_s, vc) + b_s.T[:, :, None]
    return u * sv.reshape(bsz, length, D_B)


def conv_ffn(h, w_up, cw, cb, w_down):
    z = dwconv(h @ w_up, cw, cb)
    g, u = z[..., :D_FF], z[..., D_FF:]
    return (jax.nn.silu(g) * u) @ w_down


def layer(x, mod, lp, ctx_kv, rope):
    bsz, length, _ = x.shape
    shift1, scale1, gate1, shift2, scale2, gate2 = jnp.split(mod[:, None, :], 6, axis=-1)
    h = rmsnorm(x, lp['norm1_g']) * (1 + scale1) + shift1
    z = h @ lp['w_in']
    za, zb, zq, zkv, zkr = jnp.split(z, SPLITS, axis=-1)
    a_out = conv_module(za, lp['conv_w'], lp['conv_b'], lp['conv_ln_g'], lp['conv_ln_b'])
    b_out = chunk_spatial_gate(zb, lp['sg_ln_g'], lp['sg_ln_b'], lp['sg_w'], lp['sg_b'])
    q = (rmsnorm(zq, lp['q_norm_g']) @ lp['w_uq']).reshape(bsz, length, H_C, NOPE_DIM + ROPE_DIM)
    ckv_n = rmsnorm(zkv, lp['kv_norm_g'])
    kr = zkr[:, :, None, :]
    if rope is not None:
        cos, sin = rope
        q = jnp.concatenate([q[..., :NOPE_DIM], apply_rope_2d(q[..., NOPE_DIM:], cos, sin)], axis=-1)
        kr = apply_rope_2d(kr, cos, sin)
    k, v = mla_keys_values(ckv_n, kr, lp['w_ukv'])
    if ctx_kv is not None:
        k = jnp.concatenate([ctx_kv[0], k], axis=1)
        v = jnp.concatenate([ctx_kv[1], v], axis=1)
    o = attention(q, k, v).reshape(bsz, length, H_C * V_DIM)
    mixed = jnp.concatenate([a_out, b_out, o], axis=-1) @ lp['w_out']
    x = x + gate1 * mixed
    h2 = rmsnorm(x, lp['norm2_g']) * (1 + scale2) + shift2
    x = x + gate2 * conv_ffn(h2, lp['ffn_w_up'], lp['ffn_conv_w'], lp['ffn_conv_b'], lp['ffn_w_down'])
    return x, ckv_n, zkr


def setup_inputs(seed: int = 0) -> dict:
    key = jax.random.key(seed)
    ks = jax.random.split(key, 40)
    f32 = jnp.float32

    def nrm(k, shape, scale):
        return jax.random.normal(k, shape, f32) * scale

    def gain(k, shape):
        return 1.0 + 0.02 * jax.random.normal(k, shape, f32)

    return {
        'x_prompt': nrm(ks[0], (BATCH, SEQ, D_MODEL), 1.0),
        'x_sample': nrm(ks[1], (DEC_BATCH, DEC_SEQ, D_MODEL), 1.0),
        'cache_ckv': nrm(ks[2], (DEC_BATCH, DEPTH, PAST_LEN, KV_RANK), 1.0),
        'cache_krope': nrm(ks[3], (DEC_BATCH, DEPTH, PAST_LEN, ROPE_DIM), 1.0),
        'c': nrm(ks[4], (DEC_BATCH, D_MODEL), 1.0),
        'c_ctx': nrm(ks[5], (D_MODEL,), 1.0),
        'norm1_g': gain(ks[6], (DEPTH, D_MODEL)),
        'w_ada': nrm(ks[7], (DEPTH, D_MODEL, 6 * D_MODEL), 0.5 * D_MODEL ** -0.5),
        'b_ada': nrm(ks[8], (DEPTH, 6 * D_MODEL), 0.01),
        'w_in': nrm(ks[9], (DEPTH, D_MODEL, N_IN), D_MODEL ** -0.5),
        'conv_w': nrm(ks[10], (DEPTH, CONV_K, D_A), CONV_K ** -0.5),
        'conv_b': nrm(ks[11], (DEPTH, D_A), 0.01),
        'conv_ln_g': gain(ks[12], (DEPTH, D_A)),
        'conv_ln_b': nrm(ks[13], (DEPTH, D_A), 0.01),
        'sg_ln_g': gain(ks[14], (DEPTH, D_B)),
        'sg_ln_b': nrm(ks[15], (DEPTH, D_B), 0.01),
        'sg_w': nrm(ks[16], (DEPTH, G_B, CHUNK, CHUNK), CHUNK ** -0.5),
        'sg_b': gain(ks[17], (DEPTH, G_B, CHUNK)),
        'q_norm_g': gain(ks[18], (DEPTH, Q_RANK)),
        'w_uq': nrm(ks[19], (DEPTH, Q_RANK, H_C * (NOPE_DIM + ROPE_DIM)), Q_RANK ** -0.5),
        'kv_norm_g': gain(ks[20], (DEPTH, KV_RANK)),
        'w_ukv': nrm(ks[21], (DEPTH, KV_RANK, H_C * (NOPE_DIM + V_DIM)), KV_RANK ** -0.5),
        'w_out': nrm(ks[22], (DEPTH, D_MIX, D_MODEL), D_MIX ** -0.5),
        'norm2_g': gain(ks[23], (DEPTH, D_MODEL)),
        'ffn_w_up': nrm(ks[24], (DEPTH, D_MODEL, 2 * D_FF), D_MODEL ** -0.5),
        'ffn_conv_w': nrm(ks[25], (DEPTH, FFN_K, 2 * D_FF), FFN_K ** -0.5),
        'ffn_conv_b': nrm(ks[26], (DEPTH, 2 * D_FF), 0.01),
        'ffn_w_down': nrm(ks[27], (DEPTH, D_FF, D_MODEL), D_FF ** -0.5),
        'final_g': gain(ks[28], (D_MODEL,)),
    }


def reference(x_prompt, x_sample, cache_ckv, cache_krope, c, c_ctx, norm1_g, w_ada, b_ada, w_in,
              conv_w, conv_b, conv_ln_g, conv_ln_b, sg_ln_g, sg_ln_b, sg_w, sg_b, q_norm_g, w_uq,
              kv_norm_g, w_ukv, w_out, norm2_g, ffn_w_up, ffn_conv_w, ffn_conv_b, ffn_w_down, final_g):
    rope = rope_tables(x_sample.shape[1])
    xp, xs = x_prompt, x_sample
    ckv_list, kr_list = [], []
    for l in range(DEPTH):
        lp = {
            'norm1_g': norm1_g[l], 'w_in': w_in[l], 'conv_w': conv_w[l], 'conv_b': conv_b[l],
            'conv_ln_g': conv_ln_g[l], 'conv_ln_b': conv_ln_b[l], 'sg_ln_g': sg_ln_g[l],
            'sg_ln_b': sg_ln_b[l], 'sg_w': sg_w[l], 'sg_b': sg_b[l], 'q_norm_g': q_norm_g[l],
            'w_uq': w_uq[l], 'kv_norm_g': kv_norm_g[l], 'w_ukv': w_ukv[l], 'w_out': w_out[l],
            'norm2_g': norm2_g[l], 'ffn_w_up': ffn_w_up[l], 'ffn_conv_w': ffn_conv_w[l],
            'ffn_conv_b': ffn_conv_b[l], 'ffn_w_down': ffn_w_down[l],
        }
        mod_ctx = (jax.nn.silu(c_ctx) @ w_ada[l] + b_ada[l])[None]
        xp, ckv_n, kr = layer(xp, mod_ctx, lp, None, None)
        ckv_list.append(ckv_n)
        kr_list.append(kr)
        mod_lat = jax.nn.silu(c) @ w_ada[l] + b_ada[l]
        ctx_kv = mla_keys_values(cache_ckv[:, l], cache_krope[:, l][:, :, None, :], w_ukv[l])
        xs, _, _ = layer(xs, mod_lat, lp, ctx_kv, rope)
    y_prompt = rmsnorm(xp, final_g)
    y_sample = rmsnorm(xs, final_g)
    new_ckv = jnp.stack(ckv_list, axis=1)
    new_krope = jnp.stack(kr_list, axis=1)
    return (y_prompt, y_sample, new_ckv, new_krope)
```

```python
import functools

import jax
import jax.numpy as jnp
from jax import lax
from jax.experimental import pallas as pl
from jax.experimental.pallas import tpu as pltpu

D_MODEL = 2048
DEPTH = 2
GRID_W = 64
D_A = 512
G_A = 4
CONV_K = 31
D_B = 512
G_B = 4
CHUNK = 128
NOPE_DIM = 128
ROPE_DIM = 64
V_DIM = 128
N_HEADS = 8
Q_RANK = 512
KV_RANK = 256
D_FF = 5632
FFN_K = 3
ROPE_THETA = 10000.0
EPS = 1e-6
ATTN_SCALE = (NOPE_DIM + ROPE_DIM) ** -0.5

LANE = 128
QK_DIM = 2 * LANE
HALO = 16
VMEM_LIMIT = 60 * 1024 * 1024

C_ZA = 0
C_ZB = 2 * D_A
C_ZQ = C_ZB + 2 * D_B
C_ZKV = C_ZQ + Q_RANK
C_ZKR = C_ZKV + KV_RANK
N_IN_EXT = C_ZKR + 2 * LANE

F32 = jnp.float32
BF16 = jnp.bfloat16


def _dot(a, b):
    return jnp.dot(a, b, preferred_element_type=F32)


def _rmsnorm(x, g):
    return x * lax.rsqrt(jnp.mean(x * x, axis=-1, keepdims=True) + EPS) * g


def _group_ln(x, g, b, groups):
    outs = []
    for i in range(groups):
        xg = x[:, i * LANE:(i + 1) * LANE]
        mu = jnp.mean(xg, axis=-1, keepdims=True)
        xc = xg - mu
        var = jnp.mean(xc * xc, axis=-1, keepdims=True)
        outs.append(xc * lax.rsqrt(var + EPS))
    return jnp.concatenate(outs, axis=-1) * g + b


def _sigmoid(x):
    return 1.0 / (1.0 + jnp.exp(-x))


def _const_spec(shape):
    nd = len(shape)
    return pl.BlockSpec(shape, lambda *_: (0,) * nd, pipeline_mode=pl.Buffered(1))


def _ada_kernel(c_ref, w_ref, b_ref, o_ref):
    c = c_ref[...]
    s = (c * _sigmoid(c)).astype(BF16)
    o_ref[0] = _dot(s, w_ref[0].astype(BF16)) + b_ref[0]


def _ada_mod(cvec, w_ada, b_ada):
    tn = 1024
    n = 6 * D_MODEL
    return pl.pallas_call(
        _ada_kernel,
        out_shape=jax.ShapeDtypeStruct((DEPTH, 8, n), F32),
        grid=(DEPTH, n // tn),
        in_specs=[
            pl.BlockSpec((8, D_MODEL), lambda l, j: (0, 0)),
            pl.BlockSpec((1, D_MODEL, tn), lambda l, j: (l, 0, j)),
            pl.BlockSpec((1, 1, tn), lambda l, j: (l, 0, j)),
        ],
        out_specs=pl.BlockSpec((1, 8, tn), lambda l, j: (l, 0, j)),
        compiler_params=pltpu.CompilerParams(
            dimension_semantics=("arbitrary", "arbitrary"), vmem_limit_bytes=VMEM_LIMIT),
        name="ada_mod",
    )(cvec, w_ada, b_ada.reshape(DEPTH, 1, n))


def _in_proj_kernel(*refs, tm, use_rope, emit_cache):
    (x_ref, mod_ref, g1_ref, win_ref, sgg_ref, sgb_ref, sgw_ref, sgbias_ref,
     qg_ref, wuq_ref, kvg_ref, wukv_ref) = refs[:12]
    refs = refs[12:]
    if use_rope:
        cos_ref, sin_ref = refs[:2]
        refs = refs[2:]
    a_ref, b_ref, q_ref, k_ref, v_ref = refs[:5]
    refs = refs[5:]
    if emit_cache:
        ckv_ref, kr_ref = refs

    mod = mod_ref[0]
    h = _rmsnorm(x_ref[...], g1_ref[...]) * (1.0 + mod[1:2]) + mod[0:1]
    h = h.astype(BF16)

    za = _dot(h, win_ref[:, C_ZA:C_ZA + 2 * D_A])
    a_ref[...] = (za[:, :D_A] * _sigmoid(za[:, D_A:])).astype(BF16)

    zb = _dot(h, win_ref[:, C_ZB:C_ZB + 2 * D_B])
    zb = 0.5 * zb * (1.0 + lax.erf(zb * (2.0 ** -0.5)))
    u = zb[:, :D_B]
    vn = _group_ln(zb[:, D_B:], sgg_ref[...], sgb_ref[...], G_B).astype(BF16)
    for c in range(tm // CHUNK):
        rows = slice(c * CHUNK, (c + 1) * CHUNK)
        for g in range(G_B):
            cols = slice(g * LANE, (g + 1) * LANE)
            sv = _dot(sgw_ref[g], vn[rows, cols]) + sgbias_ref[g]
            b_ref[rows, cols] = (u[rows, cols] * sv).astype(BF16)

    zq = _dot(h, win_ref[:, C_ZQ:C_ZQ + Q_RANK])
    qn = _rmsnorm(zq, qg_ref[...]).astype(BF16)
    q = _dot(qn, wuq_ref[...])
    hw = N_HEADS * LANE
    if use_rope:
        cos = cos_ref[...]
        sin = sin_ref[...]
    for hd in range(N_HEADS):
        cols = slice(hd * LANE, (hd + 1) * LANE)
        q_ref[hd, :, 0:LANE] = (q[:, cols] * ATTN_SCALE).astype(BF16)
        qr = q[:, hw:2 * hw][:, cols]
        if use_rope:
            qr = qr * cos + q[:, 2 * hw:3 * hw][:, cols] * sin
        q_ref[hd, :, LANE:QK_DIM] = (qr * ATTN_SCALE).astype(BF16)

    nkv = KV_RANK + (2 * LANE if use_rope else LANE)
    zkv = _dot(h, win_ref[:, C_ZKV:C_ZKV + nkv])
    ckv_n = _rmsnorm(zkv[:, :KV_RANK], kvg_ref[...])
    kr = zkv[:, KV_RANK:KV_RANK + LANE]
    if emit_cache:
        ckv_ref[...] = ckv_n
        kr_ref[...] = kr[:, :ROPE_DIM]
    if use_rope:
        kr = kr * cos + zkv[:, KV_RANK + LANE:KV_RANK + 2 * LANE] * sin
    kr = kr.astype(BF16)
    kv = _dot(ckv_n.astype(BF16), wukv_ref[...])
    for hd in range(N_HEADS):
        cols = slice(hd * LANE, (hd + 1) * LANE)
        k_ref[hd, :, 0:LANE] = kv[:, cols].astype(BF16)
        k_ref[hd, :, LANE:QK_DIM] = kr
        v_ref[hd] = kv[:, hw:2 * hw][:, cols].astype(BF16)


def _in_proj(x, mod, lw, *, seq_len, rope, emit_cache, tm):
    t = x.shape[0]
    use_rope = rope is not None
    tiles_per_seq = max(seq_len // tm, 1)
    n_mod = mod.shape[0]

    def mod_idx(i):
        return (jnp.minimum(i * tm // seq_len, n_mod - 1), 0, 0)

    wuq = lw["wuq_rope"] if use_rope else lw["wuq"]
    in_specs = [
        pl.BlockSpec((tm, D_MODEL), lambda i: (i, 0)),
        pl.BlockSpec((1, 6, D_MODEL), mod_idx),
        _const_spec((1, D_MODEL)),
        _const_spec((D_MODEL, N_IN_EXT)),
        _const_spec((1, D_B)), _const_spec((1, D_B)),
        _const_spec((G_B, CHUNK, CHUNK)), _const_spec((G_B, CHUNK, LANE)),
        _const_spec((1, Q_RANK)),
        _const_spec(wuq.shape),
        _const_spec((1, KV_RANK)),
        _const_spec((KV_RANK, 2 * N_HEADS * LANE)),
    ]
    args = [x, mod, lw["norm1_g"], lw["w_in"], lw["sg_ln_g"], lw["sg_ln_b"], lw["sg_w"], lw["sg_bias"],
            lw["q_norm_g"], wuq, lw["kv_norm_g"], lw["w_ukv"]]
    if use_rope:
        in_specs += [pl.BlockSpec((tm, LANE), lambda i: (i % tiles_per_seq, 0))] * 2
        args += list(rope)
    out_shape = [
        jax.ShapeDtypeStruct((t, D_A), BF16),
        jax.ShapeDtypeStruct((t, D_B), BF16),
        jax.ShapeDtypeStruct((N_HEADS, t, QK_DIM), BF16),
        jax.ShapeDtypeStruct((N_HEADS, t, QK_DIM), BF16),
        jax.ShapeDtypeStruct((N_HEADS, t, V_DIM), BF16),
    ]
    out_specs = [
        pl.BlockSpec((tm, D_A), lambda i: (i, 0)),
        pl.BlockSpec((tm, D_B), lambda i: (i, 0)),
        pl.BlockSpec((N_HEADS, tm, QK_DIM), lambda i: (0, i, 0)),
        pl.BlockSpec((N_HEADS, tm, QK_DIM), lambda i: (0, i, 0)),
        pl.BlockSpec((N_HEADS, tm, V_DIM), lambda i: (0, i, 0)),
    ]
    if emit_cache:
        out_shape += [jax.ShapeDtypeStruct((t, KV_RANK), F32), jax.ShapeDtypeStruct((t, ROPE_DIM), F32)]
        out_specs += [pl.BlockSpec((tm, KV_RANK), lambda i: (i, 0)),
                      pl.BlockSpec((tm, ROPE_DIM), lambda i: (i, 0))]
    return pl.pallas_call(
        functools.partial(_in_proj_kernel, tm=tm, use_rope=use_rope, emit_cache=emit_cache),
        out_shape=out_shape,
        grid=(t // tm,),
        in_specs=in_specs,
        out_specs=out_specs,
        compiler_params=pltpu.CompilerParams(
            dimension_semantics=("parallel",), vmem_limit_bytes=VMEM_LIMIT),
        name="in_proj",
    )(*args)


def _ctx_kv_kernel(ckv_ref, kr_ref, wukv_ref, k_ref, v_ref):
    kv = _dot(ckv_ref[...].astype(BF16), wukv_ref[...])
    kr = kr_ref[...].astype(BF16)
    hw = N_HEADS * LANE
    for hd in range(N_HEADS):
        cols = slice(hd * LANE, (hd + 1) * LANE)
        k_ref[hd, :, 0:LANE] = kv[:, cols].astype(BF16)
        k_ref[hd, :, LANE:QK_DIM] = kr
        v_ref[hd] = kv[:, hw:2 * hw][:, cols].astype(BF16)


def _ctx_kv(ckv, kr_pad, w_ukv, *, tm):
    t = ckv.shape[0]
    return pl.pallas_call(
        _ctx_kv_kernel,
        out_shape=[jax.ShapeDtypeStruct((N_HEADS, t, QK_DIM), BF16),
                   jax.ShapeDtypeStruct((N_HEADS, t, V_DIM), BF16)],
        grid=(t // tm,),
        in_specs=[pl.BlockSpec((tm, KV_RANK), lambda i: (i, 0)),
                  pl.BlockSpec((tm, LANE), lambda i: (i, 0)),
                  _const_spec((KV_RANK, 2 * N_HEADS * LANE))],
        out_specs=[pl.BlockSpec((N_HEADS, tm, QK_DIM), lambda i: (0, i, 0)),
                   pl.BlockSpec((N_HEADS, tm, V_DIM), lambda i: (0, i, 0))],
        compiler_params=pltpu.CompilerParams(
            dimension_semantics=("parallel",), vmem_limit_bytes=VMEM_LIMIT),
        name="ctx_kv",
    )(ckv, kr_pad, w_ukv)


def _attn_kernel(*refs, heads, has_ctx, kc):
    if has_ctx:
        q_ref, kl_ref, vl_ref, kx_ref, vx_ref, o_ref = refs
    else:
        q_ref, kl_ref, vl_ref, o_ref = refs
    for hd in range(heads):
        q = q_ref[hd]
        chunks = []
        if has_ctx:
            chunks.append((kx_ref, vx_ref, 0, kx_ref.shape[1]))
        n_keys = kl_ref.shape[1]
        for s0 in range(0, n_keys, kc):
            chunks.append((kl_ref, vl_ref, s0, min(kc, n_keys - s0)))
        m = l = acc = None
        for (k_r, v_r, s0, n) in chunks:
            k = k_r[hd, s0:s0 + n, :]
            v = v_r[hd, s0:s0 + n, :]
            s = lax.dot_general(q, k, (((1,), (1,)), ((), ())), preferred_element_type=F32)
            m_c = jnp.max(s, axis=-1, keepdims=True)
            if m is None:
                m = m_c
                p = jnp.exp(s - m)
                l = jnp.sum(p, axis=-1, keepdims=True)
                acc = _dot(p.astype(BF16), v)
            else:
                m_new = jnp.maximum(m, m_c)
                alpha = jnp.exp(m - m_new)
                p = jnp.exp(s - m_new)
                l = alpha * l + jnp.sum(p, axis=-1, keepdims=True)
                acc = alpha * acc + _dot(p.astype(BF16), v)
                m = m_new
        o_ref[:, hd * V_DIM:(hd + 1) * V_DIM] = (acc / l).astype(o_ref.dtype)


def _attention(q, k, v, ctx, *, batch, seq_len, tq, kc, heads_per_step):
    t = q.shape[1]
    nq = seq_len // tq
    hb = heads_per_step
    has_ctx = ctx is not None
    in_specs = [
        pl.BlockSpec((hb, tq, QK_DIM), lambda b, h, i: (h, b * nq + i, 0)),
        pl.BlockSpec((hb, seq_len, QK_DIM), lambda b, h, i: (h, b, 0)),
        pl.BlockSpec((hb, seq_len, V_DIM), lambda b, h, i: (h, b, 0)),
    ]
    args = [q, k, v]
    if has_ctx:
        past = ctx[0].shape[1] // batch
        in_specs += [pl.BlockSpec((hb, past, QK_DIM), lambda b, h, i: (h, b, 0)),
                     pl.BlockSpec((hb, past, V_DIM), lambda b, h, i: (h, b, 0))]
        args += list(ctx)
    return pl.pallas_call(
        functools.partial(_attn_kernel, heads=hb, has_ctx=has_ctx, kc=kc),
        out_shape=jax.ShapeDtypeStruct((t, N_HEADS * V_DIM), BF16),
        grid=(batch, N_HEADS // hb, nq),
        in_specs=in_specs,
        out_specs=pl.BlockSpec((tq, hb * V_DIM), lambda b, h, i: (b * nq + i, h)),
        compiler_params=pltpu.CompilerParams(
            dimension_semantics=("parallel", "parallel", "arbitrary"), vmem_limit_bytes=VMEM_LIMIT),
        name="attention",
    )(*args)


def _out_proj_kernel(x_ref, mod_ref, a_ref, ap_ref, an_ref, b_ref, o_ref, cw_ref, cb_ref, lg_ref, lb_ref,
                     wout_ref, y_ref, abuf, cat, *, tm, tiles_per_seq):
    i = pl.program_id(0)
    ti = i % tiles_per_seq
    prev_ok = jnp.where(ti > 0, 1.0, 0.0)
    next_ok = jnp.where(ti < tiles_per_seq - 1, 1.0, 0.0)
    abuf[0:HALO, :] = ap_ref[...].astype(F32) * prev_ok
    abuf[HALO:HALO + tm, :] = a_ref[...].astype(F32)
    abuf[HALO + tm:2 * HALO + tm, :] = an_ref[...].astype(F32) * next_ok

    cw = cw_ref[...]
    acc = jnp.zeros((tm, D_A), F32) + cb_ref[...]
    for kk in range(CONV_K):
        off = HALO - CONV_K // 2 + kk
        acc = acc + abuf[off:off + tm, :] * cw[kk:kk + 1, :]
    an = _group_ln(acc, lg_ref[...], lb_ref[...], G_A)
    cat[:, 0:D_A] = (an * _sigmoid(an)).astype(BF16)
    cat[:, D_A:D_A + D_B] = b_ref[...]
    cat[:, D_A + D_B:] = o_ref[...]
    mixed = _dot(cat[...], wout_ref[...])
    y_ref[...] = x_ref[...] + mod_ref[0][2:3] * mixed


def _out_proj(x, mod, a, b, o, lw, *, seq_len, tm):
    t = x.shape[0]
    tiles_per_seq = seq_len // tm
    hb = tm // HALO
    n_hb = t // HALO
    n_mod = mod.shape[0]

    def mod_idx(i):
        return (jnp.minimum(i * tm // seq_len, n_mod - 1), 0, 0)

    d_mix = D_A + D_B + N_HEADS * V_DIM
    return pl.pallas_call(
        functools.partial(_out_proj_kernel, tm=tm, tiles_per_seq=tiles_per_seq),
        out_shape=jax.ShapeDtypeStruct((t, D_MODEL), F32),
        grid=(t // tm,),
        in_specs=[
            pl.BlockSpec((tm, D_MODEL), lambda i: (i, 0)),
            pl.BlockSpec((1, 6, D_MODEL), mod_idx),
            pl.BlockSpec((tm, D_A), lambda i: (i, 0)),
            pl.BlockSpec((HALO, D_A), lambda i: (jnp.maximum(i * hb - 1, 0), 0)),
            pl.BlockSpec((HALO, D_A), lambda i: (jnp.minimum((i + 1) * hb, n_hb - 1), 0)),
            pl.BlockSpec((tm, D_B), lambda i: (i, 0)),
            pl.BlockSpec((tm, N_HEADS * V_DIM), lambda i: (i, 0)),
            _const_spec((CONV_K, D_A)), _const_spec((1, D_A)), _const_spec((1, D_A)), _const_spec((1, D_A)),
            _const_spec((d_mix, D_MODEL)),
        ],
        out_specs=pl.BlockSpec((tm, D_MODEL), lambda i: (i, 0)),
        scratch_shapes=[pltpu.VMEM((tm + 2 * HALO, D_A), F32), pltpu.VMEM((tm, d_mix), BF16)],
        compiler_params=pltpu.CompilerParams(
            dimension_semantics=("parallel",), vmem_limit_bytes=VMEM_LIMIT),
        name="out_proj",
    )(x, mod, a, a, a, b, o, lw["conv_w"], lw["conv_b"], lw["conv_ln_g"], lw["conv_ln_b"], lw["w_out"])


def _ffn_kernel(x_ref, xp_ref, xn_ref, mod_ref, g2_ref, wup_ref, cw_ref, cb_ref, wdn_ref, fg_ref,
                y_ref, hs, zs, acc, *, tm, tf, tiles_per_seq, final_norm):
    i = pl.program_id(0)
    j = pl.program_id(1)
    nj = pl.num_programs(1)
    mod = mod_ref[0]

    @pl.when(j == 0)
    def _():
        ti = i % tiles_per_seq
        prev_ok = jnp.where(ti > 0, 1.0, 0.0)
        next_ok = jnp.where(ti < tiles_per_seq - 1, 1.0, 0.0)
        g2 = g2_ref[...]
        scale = 1.0 + mod[4:5]
        shift = mod[3:4]
        hs[0:HALO, :] = ((_rmsnorm(xp_ref[...], g2) * scale + shift) * prev_ok).astype(BF16)
        hs[HALO:HALO + tm, :] = (_rmsnorm(x_ref[...], g2) * scale + shift).astype(BF16)
        hs[HALO + tm:, :] = ((_rmsnorm(xn_ref[...], g2) * scale + shift) * next_ok).astype(BF16)
        acc[...] = jnp.zeros_like(acc)

    zs[...] = _dot(hs[...], wup_ref[...])
    cw = cw_ref[...]
    zc = cb_ref[...] + zs[HALO - 1:HALO - 1 + tm, :] * cw[0:1, :]
    zc = zc + zs[HALO:HALO + tm, :] * cw[1:2, :]
    zc = zc + zs[HALO + 1:HALO + 1 + tm, :] * cw[2:3, :]
    g = zc[:, :tf]
    act = (g * _sigmoid(g) * zc[:, tf:]).astype(BF16)
    acc[...] += _dot(act, wdn_ref[...])

    @pl.when(j == nj - 1)
    def _():
        y = x_ref[...] + mod[5:6] * acc[...]
        if final_norm:
            y = _rmsnorm(y, fg_ref[...])
        y_ref[...] = y


def _ffn(x, mod, lw, final_g, *, seq_len, tm, tf, final_norm):
    t = x.shape[0]
    tiles_per_seq = seq_len // tm
    hb = tm // HALO
    n_hb = t // HALO
    n_mod = mod.shape[0]

    def mod_idx(i, j):
        return (jnp.minimum(i * tm // seq_len, n_mod - 1), 0, 0)

    return pl.pallas_call(
        functools.partial(_ffn_kernel, tm=tm, tf=tf, tiles_per_seq=tiles_per_seq, final_norm=final_norm),
        out_shape=jax.ShapeDtypeStruct((t, D_MODEL), F32),
        grid=(t // tm, D_FF // tf),
        in_specs=[
            pl.BlockSpec((tm, D_MODEL), lambda i, j: (i, 0)),
            pl.BlockSpec((HALO, D_MODEL), lambda i, j: (jnp.maximum(i * hb - 1, 0), 0)),
            pl.BlockSpec((HALO, D_MODEL), lambda i, j: (jnp.minimum((i + 1) * hb, n_hb - 1), 0)),
            pl.BlockSpec((1, 6, D_MODEL), mod_idx),
            pl.BlockSpec((1, D_MODEL), lambda i, j: (0, 0)),
            pl.BlockSpec((D_MODEL, 2 * tf), lambda i, j: (0, j)),
            pl.BlockSpec((FFN_K, 2 * tf), lambda i, j: (0, j)),
            pl.BlockSpec((1, 2 * tf), lambda i, j: (0, j)),
            pl.BlockSpec((tf, D_MODEL), lambda i, j: (j, 0)),
            pl.BlockSpec((1, D_MODEL), lambda i, j: (0, 0)),
        ],
        out_specs=pl.BlockSpec((tm, D_MODEL), lambda i, j: (i, 0)),
        scratch_shapes=[pltpu.VMEM((tm + 2 * HALO, D_MODEL), BF16),
                        pltpu.VMEM((tm + 2 * HALO, 2 * tf), F32),
                        pltpu.VMEM((tm, D_MODEL), F32)],
        compiler_params=pltpu.CompilerParams(
            dimension_semantics=("parallel", "arbitrary"), vmem_limit_bytes=VMEM_LIMIT),
        name="conv_ffn",
    )(x, x, x, mod, lw["norm2_g"], lw["w_up"], lw["ffn_conv_w"], lw["ffn_conv_b"], lw["w_down"], final_g)


def _rope_tables(length):
    pos = jnp.arange(length)
    r = (pos // GRID_W).astype(F32)
    col = (pos % GRID_W).astype(F32)
    half = ROPE_DIM // 4
    freq = 1.0 / (ROPE_THETA ** (jnp.arange(half, dtype=F32) / half))
    ar = r[:, None] * freq
    ac = col[:, None] * freq
    zeros = jnp.zeros((length, LANE - ROPE_DIM), F32)
    cos = jnp.concatenate([jnp.cos(ar), jnp.cos(ar), jnp.cos(ac), jnp.cos(ac), zeros], axis=1)
    sin = jnp.concatenate([-jnp.sin(ar), jnp.sin(ar), -jnp.sin(ac), jnp.sin(ac), zeros], axis=1)
    return cos, sin


def _swap_rope_cols(w):
    q = ROPE_DIM // 4
    return jnp.concatenate([w[..., q:2 * q], w[..., 0:q], w[..., 3 * q:4 * q], w[..., 2 * q:3 * q]], axis=-1)


def _prep_layer(p, l, tf):
    w_in = p["w_in"][l]
    zkr_w = w_in[:, C_ZKR:C_ZKR + ROPE_DIM]
    zpad = jnp.zeros((D_MODEL, LANE - ROPE_DIM), F32)
    w_in_ext = jnp.concatenate([w_in[:, :C_ZKR], zkr_w, zpad, _swap_rope_cols(zkr_w), zpad], axis=1)

    wuq = p["w_uq"][l].reshape(Q_RANK, N_HEADS, NOPE_DIM + ROPE_DIM)
    nope = wuq[:, :, :NOPE_DIM].reshape(Q_RANK, N_HEADS * NOPE_DIM)
    rope = wuq[:, :, NOPE_DIM:]
    rpad = jnp.zeros((Q_RANK, N_HEADS, LANE - ROPE_DIM), F32)
    rope_p = jnp.concatenate([rope, rpad], axis=-1).reshape(Q_RANK, N_HEADS * LANE)
    rope_s = jnp.concatenate([_swap_rope_cols(rope), rpad], axis=-1).reshape(Q_RANK, N_HEADS * LANE)

    wukv = p["w_ukv"][l].reshape(KV_RANK, N_HEADS, NOPE_DIM + V_DIM)
    wukv = jnp.concatenate([wukv[:, :, :NOPE_DIM].reshape(KV_RANK, -1),
                            wukv[:, :, NOPE_DIM:].reshape(KV_RANK, -1)], axis=1)

    nf = D_FF // tf

    def regroup(w):
        lead = w.shape[:-1]
        w2 = w.reshape(lead + (2, nf, tf))
        return jnp.swapaxes(w2, -3, -2).reshape(lead + (2 * D_FF,))

    return {
        "norm1_g": p["norm1_g"][l][None],
        "w_in": w_in_ext.astype(BF16),
        "sg_ln_g": p["sg_ln_g"][l][None], "sg_ln_b": p["sg_ln_b"][l][None],
        "sg_w": p["sg_w"][l].astype(BF16),
        "sg_bias": jnp.broadcast_to(p["sg_b"][l][:, :, None], (G_B, CHUNK, LANE)),
        "q_norm_g": p["q_norm_g"][l][None],
        "wuq": jnp.concatenate([nope, rope_p], axis=1).astype(BF16),
        "wuq_rope": jnp.concatenate([nope, rope_p, rope_s], axis=1).astype(BF16),
        "kv_norm_g": p["kv_norm_g"][l][None],
        "w_ukv": wukv.astype(BF16),
        "conv_w": p["conv_w"][l], "conv_b": p["conv_b"][l][None],
        "conv_ln_g": p["conv_ln_g"][l][None], "conv_ln_b": p["conv_ln_b"][l][None],
        "w_out": p["w_out"][l].astype(BF16),
        "norm2_g": p["norm2_g"][l][None],
        "w_up": regroup(p["ffn_w_up"][l]).astype(BF16),
        "ffn_conv_w": regroup(p["ffn_conv_w"][l]),
        "ffn_conv_b": regroup(p["ffn_conv_b"][l])[None],
        "w_down": p["ffn_w_down"][l].astype(BF16),
    }


TM_IN = 512
TM_OUT_CTX = 256
TM_OUT_LAT = 512
TM_FFN_CTX = 256
TM_FFN_LAT = 512
TF_FFN = 512
TQ_LAT = 512
KC_LAT = 1024


def kernel(x_prompt, x_sample, cache_ckv, cache_krope, c, c_ctx, norm1_g, w_ada, b_ada, w_in, conv_w, conv_b, conv_ln_g, conv_ln_b, sg_ln_g, sg_ln_b, sg_w, sg_b, q_norm_g, w_uq, kv_norm_g, w_ukv, w_out, norm2_g, ffn_w_up, ffn_conv_w, ffn_conv_b, ffn_w_down, final_g):
    p = dict(norm1_g=norm1_g, w_in=w_in, conv_w=conv_w, conv_b=conv_b, conv_ln_g=conv_ln_g,
             conv_ln_b=conv_ln_b, sg_ln_g=sg_ln_g, sg_ln_b=sg_ln_b, sg_w=sg_w, sg_b=sg_b,
             q_norm_g=q_norm_g, w_uq=w_uq, kv_norm_g=kv_norm_g, w_ukv=w_ukv, w_out=w_out,
             norm2_g=norm2_g, ffn_w_up=ffn_w_up, ffn_conv_w=ffn_conv_w, ffn_conv_b=ffn_conv_b,
             ffn_w_down=ffn_w_down)
    bp, lp_, _ = x_prompt.shape
    bs, ls, _ = x_sample.shape
    past = cache_ckv.shape[2]

    cvec = jnp.concatenate([c_ctx[None], c, jnp.zeros((8 - 1 - bs, D_MODEL), F32)], axis=0)
    mod_all = _ada_mod(cvec, w_ada, b_ada).reshape(DEPTH, 8, 6, D_MODEL)
    rope = _rope_tables(ls)
    fg = final_g[None]

    xp = x_prompt.reshape(bp * lp_, D_MODEL)
    xs = x_sample.reshape(bs * ls, D_MODEL)
    ckv_out, kr_out = [], []
    for l in range(DEPTH):
        lw = _prep_layer(p, l, TF_FFN)
        last = l == DEPTH - 1
        mod_ctx = mod_all[l, 0:1]
        mod_lat = mod_all[l, 1:1 + bs]

        a, b, q, k, v, ckv_n, kr = _in_proj(xp, mod_ctx, lw, seq_len=lp_, rope=None, emit_cache=True, tm=TM_IN)
        ckv_out.append(ckv_n.reshape(bp, lp_, KV_RANK))
        kr_out.append(kr.reshape(bp, lp_, ROPE_DIM))
        o = _attention(q, k, v, None, batch=bp, seq_len=lp_, tq=lp_, kc=lp_, heads_per_step=N_HEADS)
        xp = _out_proj(xp, mod_ctx, a, b, o, lw, seq_len=lp_, tm=TM_OUT_CTX)
        xp = _ffn(xp, mod_ctx, lw, fg, seq_len=lp_, tm=TM_FFN_CTX, tf=TF_FFN, final_norm=last)

        kr_pad = jnp.pad(cache_krope[:, l].reshape(bs * past, ROPE_DIM), ((0, 0), (0, LANE - ROPE_DIM)))
        ctx = _ctx_kv(cache_ckv[:, l].reshape(bs * past, KV_RANK), kr_pad, lw["w_ukv"], tm=past)
        a, b, q, k, v = _in_proj(xs, mod_lat, lw, seq_len=ls, rope=rope, emit_cache=False, tm=TM_IN)
        o = _attention(q, k, v, ctx, batch=bs, seq_len=ls, tq=TQ_LAT, kc=KC_LAT, heads_per_step=1)
        xs = _out_proj(xs, mod_lat, a, b, o, lw, seq_len=ls, tm=TM_OUT_LAT)
        xs = _ffn(xs, mod_lat, lw, fg, seq_len=ls, tm=TM_FFN_LAT, tf=TF_FFN, final_norm=last)

    return (xp.reshape(bp, lp_, D_MODEL), xs.reshape(bs, ls, D_MODEL),
            jnp.stack(ckv_out, axis=1), jnp.stack(kr_out, axis=1))
```

```python
import functools

import jax
import jax.numpy as jnp
from jax import lax
from jax.experimental import pallas as pl
from jax.experimental.pallas import tpu as pltpu

D_MODEL = 2048
DEPTH = 2
GRID_W = 64
D_A = 512
G_A = 4
CONV_K = 31
D_B = 512
G_B = 4
CHUNK = 128
NOPE_DIM = 128
ROPE_DIM = 64
V_DIM = 128
N_HEADS = 8
Q_RANK = 512
KV_RANK = 256
D_FF = 5632
FFN_K = 3
ROPE_THETA = 10000.0
EPS = 1e-6
ATTN_SCALE = (NOPE_DIM + ROPE_DIM) ** -0.5

LANE = 128
SUBLANE = 8
QK_DIM = 2 * LANE
HALO = 16
VMEM_LIMIT = 60 * 1024 * 1024

C_ZA = 0
C_ZB = 2 * D_A
C_ZQ = C_ZB + 2 * D_B
C_ZKV = C_ZQ + Q_RANK
C_ZKR = C_ZKV + KV_RANK
N_IN_EXT = C_ZKR + 2 * LANE
HW = N_HEADS * LANE

F32 = jnp.float32
BF16 = jnp.bfloat16


def _dot(a, b):
    return jnp.dot(a, b, preferred_element_type=F32)


def _rmsnorm(x, g):
    return x * lax.rsqrt(jnp.mean(x * x, axis=-1, keepdims=True) + EPS) * g


def _group_ln(x, g, b, groups):
    outs = []
    for i in range(groups):
        xg = x[:, i * LANE:(i + 1) * LANE]
        mu = jnp.mean(xg, axis=-1, keepdims=True)
        xc = xg - mu
        var = jnp.mean(xc * xc, axis=-1, keepdims=True)
        outs.append(xc * lax.rsqrt(var + EPS))
    return jnp.concatenate(outs, axis=-1) * g + b


def _sigmoid(x):
    return 1.0 / (1.0 + jnp.exp(-x))


def _layer_spec(shape, l):
    nd = len(shape)
    return pl.BlockSpec((None,) + tuple(shape), lambda *_: (l,) + (0,) * nd, pipeline_mode=pl.Buffered(1))


def _params(*sem):
    return pltpu.CompilerParams(dimension_semantics=sem, vmem_limit_bytes=VMEM_LIMIT)


def _ada_kernel(c_ref, w_ref, b_ref, o_ref):
    c = c_ref[...]
    s = (c * _sigmoid(c)).astype(BF16)
    o_ref[0] = _dot(s, w_ref[0].astype(BF16)) + b_ref[0]


def _ada_mod(cvec, w_ada, b_ada):
    tn = 1024
    n = 6 * D_MODEL
    return pl.pallas_call(
        _ada_kernel,
        out_shape=jax.ShapeDtypeStruct((DEPTH, 8, n), F32),
        grid=(DEPTH, n // tn),
        in_specs=[
            pl.BlockSpec((8, D_MODEL), lambda l, j: (0, 0)),
            pl.BlockSpec((1, D_MODEL, tn), lambda l, j: (l, 0, j)),
            pl.BlockSpec((1, 1, tn), lambda l, j: (l, 0, j)),
        ],
        out_specs=pl.BlockSpec((1, 8, tn), lambda l, j: (l, 0, j)),
        compiler_params=_params("arbitrary", "arbitrary"),
        name="ada_mod",
    )(cvec, w_ada, b_ada.reshape(DEPTH, 1, n))


def _in_proj_kernel(*refs, tm, use_rope, emit_cache):
    (x_ref, mod_ref, g1_ref, win_ref, sgg_ref, sgb_ref, sgw_ref, sgbias_ref,
     qg_ref, wuq_ref, kvg_ref, wukv_ref) = refs[:12]
    refs = refs[12:]
    if use_rope:
        cos_ref, sin_ref = refs[:2]
        refs = refs[2:]
    a_ref, b_ref, q_ref, k_ref, v_ref = refs[:5]
    refs = refs[5:]
    if emit_cache:
        ckv_ref, kr_ref = refs

    mod = mod_ref[0]
    h = _rmsnorm(x_ref[...], g1_ref[...]) * (1.0 + mod[1:2]) + mod[0:1]
    h = h.astype(BF16)

    za = _dot(h, win_ref[:, C_ZA:C_ZA + 2 * D_A])
    a_ref[...] = (za[:, :D_A] * _sigmoid(za[:, D_A:])).astype(BF16)

    zb = _dot(h, win_ref[:, C_ZB:C_ZB + 2 * D_B])
    zb = 0.5 * zb * (1.0 + lax.erf(zb * (2.0 ** -0.5)))
    u = zb[:, :D_B]
    vn = _group_ln(zb[:, D_B:], sgg_ref[...], sgb_ref[...], G_B).astype(BF16)
    for c in range(tm // CHUNK):
        rows = slice(c * CHUNK, (c + 1) * CHUNK)
        for g in range(G_B):
            cols = slice(g * LANE, (g + 1) * LANE)
            sv = _dot(sgw_ref[g], vn[rows, cols]) + sgbias_ref[g]
            b_ref[rows, cols] = (u[rows, cols] * sv).astype(BF16)

    zq = _dot(h, win_ref[:, C_ZQ:C_ZQ + Q_RANK])
    qn = _rmsnorm(zq, qg_ref[...]).astype(BF16)
    q = _dot(qn, wuq_ref[...])
    if use_rope:
        cos = cos_ref[...]
        sin = sin_ref[...]
    for hd in range(N_HEADS):
        cols = slice(hd * LANE, (hd + 1) * LANE)
        q_ref[hd, :, 0:LANE] = (q[:, cols] * ATTN_SCALE).astype(BF16)
        qr = q[:, HW:2 * HW][:, cols]
        if use_rope:
            qr = qr * cos + q[:, 2 * HW:3 * HW][:, cols] * sin
        q_ref[hd, :, LANE:QK_DIM] = (qr * ATTN_SCALE).astype(BF16)

    nkv = KV_RANK + (2 * LANE if use_rope else LANE)
    zkv = _dot(h, win_ref[:, C_ZKV:C_ZKV + nkv])
    ckv_n = _rmsnorm(zkv[:, :KV_RANK], kvg_ref[...])
    kr = zkv[:, KV_RANK:KV_RANK + LANE]
    if emit_cache:
        ckv_ref[...] = ckv_n
        kr_ref[...] = kr[:, :ROPE_DIM]
    if use_rope:
        kr = kr * cos + zkv[:, KV_RANK + LANE:KV_RANK + 2 * LANE] * sin
    kr = kr.astype(BF16)
    kv = _dot(ckv_n.astype(BF16), wukv_ref[...])
    for hd in range(N_HEADS):
        cols = slice(hd * LANE, (hd + 1) * LANE)
        k_ref[hd, :, 0:LANE] = kv[:, cols].astype(BF16)
        k_ref[hd, :, LANE:QK_DIM] = kr
        v_ref[hd] = kv[:, HW:2 * HW][:, cols].astype(BF16)


def _in_proj(x, mod, w, l, *, seq_len, rope, emit_cache, tm):
    t = x.shape[0]
    use_rope = rope is not None
    tiles_per_seq = max(seq_len // tm, 1)
    n_mod = mod.shape[0]

    def mod_idx(i):
        return (jnp.minimum(i * tm // seq_len, n_mod - 1), 0, 0)

    n_uq = 3 * HW if use_rope else 2 * HW
    in_specs = [
        pl.BlockSpec((tm, D_MODEL), lambda i: (i, 0)),
        pl.BlockSpec((1, 6, D_MODEL), mod_idx),
        _layer_spec((1, D_MODEL), l),
        _layer_spec((D_MODEL, N_IN_EXT), l),
        _layer_spec((1, D_B), l), _layer_spec((1, D_B), l),
        _layer_spec((G_B, CHUNK, CHUNK), l), _layer_spec((G_B, CHUNK, LANE), l),
        _layer_spec((1, Q_RANK), l),
        _layer_spec((Q_RANK, n_uq), l),
        _layer_spec((1, KV_RANK), l),
        _layer_spec((KV_RANK, 2 * HW), l),
    ]
    args = [x, mod, w["norm1_g"], w["w_in"], w["sg_ln_g"], w["sg_ln_b"], w["sg_w"], w["sg_bias"],
            w["q_norm_g"], w["w_uq"], w["kv_norm_g"], w["w_ukv"]]
    if use_rope:
        in_specs += [pl.BlockSpec((tm, LANE), lambda i: (i % tiles_per_seq, 0))] * 2
        args += list(rope)
    out_shape = [
        jax.ShapeDtypeStruct((t, D_A), BF16),
        jax.ShapeDtypeStruct((t, D_B), BF16),
        jax.ShapeDtypeStruct((N_HEADS, t, QK_DIM), BF16),
        jax.ShapeDtypeStruct((N_HEADS, t, QK_DIM), BF16),
        jax.ShapeDtypeStruct((N_HEADS, t, V_DIM), BF16),
    ]
    out_specs = [
        pl.BlockSpec((tm, D_A), lambda i: (i, 0)),
        pl.BlockSpec((tm, D_B), lambda i: (i, 0)),
        pl.BlockSpec((N_HEADS, tm, QK_DIM), lambda i: (0, i, 0)),
        pl.BlockSpec((N_HEADS, tm, QK_DIM), lambda i: (0, i, 0)),
        pl.BlockSpec((N_HEADS, tm, V_DIM), lambda i: (0, i, 0)),
    ]
    if emit_cache:
        out_shape += [jax.ShapeDtypeStruct((t, KV_RANK), F32), jax.ShapeDtypeStruct((t, ROPE_DIM), F32)]
        out_specs += [pl.BlockSpec((tm, KV_RANK), lambda i: (i, 0)),
                      pl.BlockSpec((tm, ROPE_DIM), lambda i: (i, 0))]
    return pl.pallas_call(
        functools.partial(_in_proj_kernel, tm=tm, use_rope=use_rope, emit_cache=emit_cache),
        out_shape=out_shape,
        grid=(t // tm,),
        in_specs=in_specs,
        out_specs=out_specs,
        compiler_params=_params("parallel"),
        name="in_proj",
    )(*args)


def _ctx_kv_kernel(ckv_ref, kr_ref, wukv_ref, k_ref, v_ref):
    kv = _dot(ckv_ref[...].astype(BF16), wukv_ref[...])
    kr = kr_ref[...].astype(BF16)
    zeros = jnp.zeros((kr.shape[0], LANE - ROPE_DIM), BF16)
    for hd in range(N_HEADS):
        cols = slice(hd * LANE, (hd + 1) * LANE)
        k_ref[hd, :, 0:LANE] = kv[:, cols].astype(BF16)
        k_ref[hd, :, LANE:LANE + ROPE_DIM] = kr
        k_ref[hd, :, LANE + ROPE_DIM:QK_DIM] = zeros
        v_ref[hd] = kv[:, HW:2 * HW][:, cols].astype(BF16)


def _ctx_kv(cache_ckv, cache_krope, w, l):
    bsz, _, past, _ = cache_ckv.shape
    return pl.pallas_call(
        _ctx_kv_kernel,
        out_shape=[jax.ShapeDtypeStruct((N_HEADS, bsz * past, QK_DIM), BF16),
                   jax.ShapeDtypeStruct((N_HEADS, bsz * past, V_DIM), BF16)],
        grid=(bsz,),
        in_specs=[pl.BlockSpec((None, None, past, KV_RANK), lambda b: (b, l, 0, 0)),
                  pl.BlockSpec((None, None, past, ROPE_DIM), lambda b: (b, l, 0, 0)),
                  _layer_spec((KV_RANK, 2 * HW), l)],
        out_specs=[pl.BlockSpec((N_HEADS, past, QK_DIM), lambda b: (0, b, 0)),
                   pl.BlockSpec((N_HEADS, past, V_DIM), lambda b: (0, b, 0))],
        compiler_params=_params("parallel"),
        name="ctx_kv",
    )(cache_ckv, cache_krope, w["w_ukv"])


def _attn_kernel(*refs, heads, has_ctx, kc):
    if has_ctx:
        q_ref, kl_ref, vl_ref, kx_ref, vx_ref, o_ref = refs
    else:
        q_ref, kl_ref, vl_ref, o_ref = refs
    for hd in range(heads):
        q = q_ref[hd]
        chunks = []
        if has_ctx:
            chunks.append((kx_ref, vx_ref, 0, kx_ref.shape[1]))
        n_keys = kl_ref.shape[1]
        for s0 in range(0, n_keys, kc):
            chunks.append((kl_ref, vl_ref, s0, min(kc, n_keys - s0)))
        m = l = acc = None
        for (k_r, v_r, s0, n) in chunks:
            k = k_r[hd, s0:s0 + n, :]
            v = v_r[hd, s0:s0 + n, :]
            s = lax.dot_general(q, k, (((1,), (1,)), ((), ())), preferred_element_type=F32)
            m_c = jnp.max(s, axis=-1, keepdims=True)
            if m is None:
                m = m_c
                p = jnp.exp(s - m)
                l = jnp.sum(p, axis=-1, keepdims=True)
                acc = _dot(p.astype(BF16), v)
            else:
                m_new = jnp.maximum(m, m_c)
                alpha = jnp.exp(m - m_new)
                p = jnp.exp(s - m_new)
                l = alpha * l + jnp.sum(p, axis=-1, keepdims=True)
                acc = alpha * acc + _dot(p.astype(BF16), v)
                m = m_new
        o_ref[:, hd * V_DIM:(hd + 1) * V_DIM] = (acc / l).astype(o_ref.dtype)


def _attention(q, k, v, ctx, *, batch, seq_len, tq, kc, heads_per_step):
    t = q.shape[1]
    nq = seq_len // tq
    hb = heads_per_step
    has_ctx = ctx is not None
    in_specs = [
        pl.BlockSpec((hb, tq, QK_DIM), lambda b, h, i: (h, b * nq + i, 0)),
        pl.BlockSpec((hb, seq_len, QK_DIM), lambda b, h, i: (h, b, 0)),
        pl.BlockSpec((hb, seq_len, V_DIM), lambda b, h, i: (h, b, 0)),
    ]
    args = [q, k, v]
    if has_ctx:
        past = ctx[0].shape[1] // batch
        in_specs += [pl.BlockSpec((hb, past, QK_DIM), lambda b, h, i: (h, b, 0)),
                     pl.BlockSpec((hb, past, V_DIM), lambda b, h, i: (h, b, 0))]
        args += list(ctx)
    return pl.pallas_call(
        functools.partial(_attn_kernel, heads=hb, has_ctx=has_ctx, kc=kc),
        out_shape=jax.ShapeDtypeStruct((t, N_HEADS * V_DIM), BF16),
        grid=(batch, N_HEADS // hb, nq),
        in_specs=in_specs,
        out_specs=pl.BlockSpec((tq, hb * V_DIM), lambda b, h, i: (b * nq + i, h)),
        compiler_params=_params("parallel", "parallel", "arbitrary"),
        name="attention",
    )(*args)


def _out_proj_kernel(x_ref, mod_ref, a_ref, ap_ref, an_ref, b_ref, o_ref, cw_ref, cb_ref, lg_ref, lb_ref,
                     wout_ref, y_ref, abuf, cat, *, tm, tiles_per_seq):
    i = pl.program_id(0)
    ti = i % tiles_per_seq
    prev_ok = jnp.where(ti > 0, 1.0, 0.0)
    next_ok = jnp.where(ti < tiles_per_seq - 1, 1.0, 0.0)
    abuf[0:HALO, :] = ap_ref[...].astype(F32) * prev_ok
    abuf[HALO:HALO + tm, :] = a_ref[...].astype(F32)
    abuf[HALO + tm:2 * HALO + tm, :] = an_ref[...].astype(F32) * next_ok

    cw = cw_ref[...]
    acc = jnp.zeros((tm, D_A), F32) + cb_ref[...]
    for kk in range(CONV_K):
        off = HALO - CONV_K // 2 + kk
        acc = acc + abuf[off:off + tm, :] * cw[kk:kk + 1, :]
    an = _group_ln(acc, lg_ref[...], lb_ref[...], G_A)
    cat[:, 0:D_A] = (an * _sigmoid(an)).astype(BF16)
    cat[:, D_A:D_A + D_B] = b_ref[...]
    cat[:, D_A + D_B:] = o_ref[...]
    mixed = _dot(cat[...], wout_ref[...])
    y_ref[...] = x_ref[...] + mod_ref[0][2:3] * mixed


def _out_proj(x, mod, a, b, o, w, l, *, seq_len, tm):
    t = x.shape[0]
    tiles_per_seq = seq_len // tm
    hb = tm // HALO
    n_hb = t // HALO
    n_mod = mod.shape[0]

    def mod_idx(i):
        return (jnp.minimum(i * tm // seq_len, n_mod - 1), 0, 0)

    d_mix = D_A + D_B + N_HEADS * V_DIM
    return pl.pallas_call(
        functools.partial(_out_proj_kernel, tm=tm, tiles_per_seq=tiles_per_seq),
        out_shape=jax.ShapeDtypeStruct((t, D_MODEL), F32),
        grid=(t // tm,),
        in_specs=[
            pl.BlockSpec((tm, D_MODEL), lambda i: (i, 0)),
            pl.BlockSpec((1, 6, D_MODEL), mod_idx),
            pl.BlockSpec((tm, D_A), lambda i: (i, 0)),
            pl.BlockSpec((HALO, D_A), lambda i: (jnp.maximum(i * hb - 1, 0), 0)),
            pl.BlockSpec((HALO, D_A), lambda i: (jnp.minimum((i + 1) * hb, n_hb - 1), 0)),
            pl.BlockSpec((tm, D_B), lambda i: (i, 0)),
            pl.BlockSpec((tm, N_HEADS * V_DIM), lambda i: (i, 0)),
            _layer_spec((CONV_K, D_A), l), _layer_spec((1, D_A), l), _layer_spec((1, D_A), l),
            _layer_spec((1, D_A), l),
            _layer_spec((d_mix, D_MODEL), l),
        ],
        out_specs=pl.BlockSpec((tm, D_MODEL), lambda i: (i, 0)),
        scratch_shapes=[pltpu.VMEM((tm + 2 * HALO, D_A), F32), pltpu.VMEM((tm, d_mix), BF16)],
        compiler_params=_params("parallel"),
        name="out_proj",
    )(x, mod, a, a, a, b, o, w["conv_w"], w["conv_b"], w["conv_ln_g"], w["conv_ln_b"], w["w_out"])


def _ffn_kernel(*refs, tm, seq_len, halo, final_norm):
    x_ref = refs[0]
    refs = refs[1:]
    if halo:
        xp_ref, xn_ref = refs[:2]
        refs = refs[2:]
    (mod_ref, g2_ref, wg_ref, wv_ref, cwg_ref, cwv_ref, cbg_ref, cbv_ref, wdn_ref, fg_ref,
     y_ref, hs, zg, zv) = refs
    i = pl.program_id(0)
    j = pl.program_id(1)
    nj = pl.num_programs(1)
    mod = mod_ref[0]
    pad = HALO if halo else SUBLANE
    tf = zg.shape[1]

    @pl.when(j == 0)
    def _():
        g2 = g2_ref[...]
        scale = 1.0 + mod[4:5]
        shift = mod[3:4]
        x = x_ref[...]
        if halo:
            tiles_per_seq = seq_len // tm
            ti = i % tiles_per_seq
            prev_ok = jnp.where(ti > 0, 1.0, 0.0)
            next_ok = jnp.where(ti < tiles_per_seq - 1, 1.0, 0.0)
            hs[0:HALO, :] = ((_rmsnorm(xp_ref[...], g2) * scale + shift) * prev_ok).astype(BF16)
            hs[HALO:HALO + tm, :] = (_rmsnorm(x, g2) * scale + shift).astype(BF16)
            hs[HALO + tm:, :] = ((_rmsnorm(xn_ref[...], g2) * scale + shift) * next_ok).astype(BF16)
        else:
            hs[...] = (_rmsnorm(x, g2) * scale + shift).astype(BF16)
            zeros = jnp.zeros((pad, tf), F32)
            for z in (zg, zv):
                z[0:pad, :] = zeros
                z[pad + tm:, :] = zeros
        y_ref[...] = x

    if halo:
        zg[...] = _dot(hs[...], wg_ref[...])
        zv[...] = _dot(hs[...], wv_ref[...])
    else:
        zg[pad:pad + tm, :] = _dot(hs[...], wg_ref[...])
        zv[pad:pad + tm, :] = _dot(hs[...], wv_ref[...])
        pos = lax.rem(lax.broadcasted_iota(jnp.int32, (tm, 1), 0), seq_len)
        not_first = pos != 0
        not_last = pos != seq_len - 1

    def conv(z, cw_ref, cb_ref):
        cw = cw_ref[...]
        zp = z[pad - 1:pad - 1 + tm, :]
        zn = z[pad + 1:pad + 1 + tm, :]
        if not halo:
            zp = jnp.where(not_first, zp, 0.0)
            zn = jnp.where(not_last, zn, 0.0)
        return cb_ref[...] + zp * cw[0:1, :] + z[pad:pad + tm, :] * cw[1:2, :] + zn * cw[2:3, :]

    g = conv(zg, cwg_ref, cbg_ref)
    act = (g * _sigmoid(g) * conv(zv, cwv_ref, cbv_ref)).astype(BF16)
    y_ref[...] += mod[5:6] * _dot(act, wdn_ref[...])

    if final_norm:
        @pl.when(j == nj - 1)
        def _():
            y_ref[...] = _rmsnorm(y_ref[...], fg_ref[...])


def _ffn(x, mod, w, l, final_g, *, seq_len, tm, tf, final_norm):
    t = x.shape[0]
    halo = tm < seq_len
    assert (seq_len % tm == 0) if halo else (tm % seq_len == 0)
    nf = D_FF // tf
    hb = tm // HALO
    n_hb = t // HALO
    n_mod = mod.shape[0]
    pad = HALO if halo else SUBLANE

    def mod_idx(i, j):
        return (jnp.minimum(i * tm // seq_len, n_mod - 1), 0, 0)

    def lspec(shape, idx):
        return pl.BlockSpec((None,) + shape, lambda i, j: (l,) + idx(i, j))

    in_specs = [pl.BlockSpec((tm, D_MODEL), lambda i, j: (i, 0), pipeline_mode=pl.Buffered(1))]
    args = [x]
    if halo:
        in_specs += [
            pl.BlockSpec((HALO, D_MODEL), lambda i, j: (jnp.maximum(i * hb - 1, 0), 0)),
            pl.BlockSpec((HALO, D_MODEL), lambda i, j: (jnp.minimum((i + 1) * hb, n_hb - 1), 0)),
        ]
        args += [x, x]
    in_specs += [
        pl.BlockSpec((1, 6, D_MODEL), mod_idx),
        _layer_spec((1, D_MODEL), l),
        lspec((D_MODEL, tf), lambda i, j: (0, j)),
        lspec((D_MODEL, tf), lambda i, j: (0, nf + j)),
        lspec((FFN_K, tf), lambda i, j: (0, j)),
        lspec((FFN_K, tf), lambda i, j: (0, nf + j)),
        lspec((1, tf), lambda i, j: (0, j)),
        lspec((1, tf), lambda i, j: (0, nf + j)),
        lspec((tf, D_MODEL), lambda i, j: (j, 0)),
        pl.BlockSpec((1, D_MODEL), lambda i, j: (0, 0)),
    ]
    args += [mod, w["norm2_g"], w["w_up"], w["w_up"], w["ffn_conv_w"], w["ffn_conv_w"],
             w["ffn_conv_b"], w["ffn_conv_b"], w["w_down"], final_g]
    h_rows = tm + 2 * HALO if halo else tm
    return pl.pallas_call(
        functools.partial(_ffn_kernel, tm=tm, seq_len=seq_len, halo=halo, final_norm=final_norm),
        out_shape=jax.ShapeDtypeStruct((t, D_MODEL), F32),
        grid=(t // tm, nf),
        in_specs=in_specs,
        out_specs=pl.BlockSpec((tm, D_MODEL), lambda i, j: (i, 0)),
        scratch_shapes=[pltpu.VMEM((h_rows, D_MODEL), BF16),
                        pltpu.VMEM((tm + 2 * pad, tf), F32),
                        pltpu.VMEM((tm + 2 * pad, tf), F32)],
        compiler_params=_params("parallel", "arbitrary"),
        name="conv_ffn",
    )(*args)


def _rope_tables(length):
    pos = jnp.arange(length)
    r = (pos // GRID_W).astype(F32)
    col = (pos % GRID_W).astype(F32)
    half = ROPE_DIM // 4
    freq = 1.0 / (ROPE_THETA ** (jnp.arange(half, dtype=F32) / half))
    ar = r[:, None] * freq
    ac = col[:, None] * freq
    zeros = jnp.zeros((length, LANE - ROPE_DIM), F32)
    cos = jnp.concatenate([jnp.cos(ar), jnp.cos(ar), jnp.cos(ac), jnp.cos(ac), zeros], axis=1)
    sin = jnp.concatenate([-jnp.sin(ar), jnp.sin(ar), -jnp.sin(ac), jnp.sin(ac), zeros], axis=1)
    return cos, sin


def _swap_rope_cols(w):
    q = ROPE_DIM // 4
    return jnp.concatenate([w[..., q:2 * q], w[..., 0:q], w[..., 3 * q:4 * q], w[..., 2 * q:3 * q]], axis=-1)


def _prep_weights(p):
    w_in = p["w_in"]
    zkr_w = w_in[:, :, C_ZKR:C_ZKR + ROPE_DIM]
    zpad = jnp.zeros((DEPTH, D_MODEL, LANE - ROPE_DIM), F32)
    w_in_ext = jnp.concatenate([w_in[:, :, :C_ZKR], zkr_w, zpad, _swap_rope_cols(zkr_w), zpad], axis=-1)

    wuq = p["w_uq"].reshape(DEPTH, Q_RANK, N_HEADS, NOPE_DIM + ROPE_DIM)
    nope = wuq[..., :NOPE_DIM].reshape(DEPTH, Q_RANK, HW)
    rope = wuq[..., NOPE_DIM:]
    rpad = jnp.zeros((DEPTH, Q_RANK, N_HEADS, LANE - ROPE_DIM), F32)
    rope_p = jnp.concatenate([rope, rpad], axis=-1).reshape(DEPTH, Q_RANK, HW)
    rope_s = jnp.concatenate([_swap_rope_cols(rope), rpad], axis=-1).reshape(DEPTH, Q_RANK, HW)

    wukv = p["w_ukv"].reshape(DEPTH, KV_RANK, N_HEADS, NOPE_DIM + V_DIM)
    wukv = jnp.concatenate([wukv[..., :NOPE_DIM].reshape(DEPTH, KV_RANK, HW),
                            wukv[..., NOPE_DIM:].reshape(DEPTH, KV_RANK, HW)], axis=-1)

    def vec(name):
        return p[name][:, None, :]

    return {
        "norm1_g": vec("norm1_g"),
        "w_in": w_in_ext.astype(BF16),
        "sg_ln_g": vec("sg_ln_g"), "sg_ln_b": vec("sg_ln_b"),
        "sg_w": p["sg_w"].astype(BF16),
        "sg_bias": jnp.broadcast_to(p["sg_b"][..., None], (DEPTH, G_B, CHUNK, LANE)),
        "q_norm_g": vec("q_norm_g"),
        "w_uq": jnp.concatenate([nope, rope_p, rope_s], axis=-1).astype(BF16),
        "kv_norm_g": vec("kv_norm_g"),
        "w_ukv": wukv.astype(BF16),
        "conv_w": p["conv_w"], "conv_b": vec("conv_b"),
        "conv_ln_g": vec("conv_ln_g"), "conv_ln_b": vec("conv_ln_b"),
        "w_out": p["w_out"].astype(BF16),
        "norm2_g": vec("norm2_g"),
        "w_up": p["ffn_w_up"].astype(BF16),
        "ffn_conv_w": p["ffn_conv_w"], "ffn_conv_b": vec("ffn_conv_b"),
        "w_down": p["ffn_w_down"].astype(BF16),
    }


TM_IN = 512
TM_OUT_CTX = 256
TM_OUT_LAT = 512
TM_FFN = 1024
TF_FFN = 512
TQ_LAT = 512
KC_LAT = 1024


def kernel(x_prompt, x_sample, cache_ckv, cache_krope, c, c_ctx, norm1_g, w_ada, b_ada, w_in, conv_w, conv_b, conv_ln_g, conv_ln_b, sg_ln_g, sg_ln_b, sg_w, sg_b, q_norm_g, w_uq, kv_norm_g, w_ukv, w_out, norm2_g, ffn_w_up, ffn_conv_w, ffn_conv_b, ffn_w_down, final_g):
    w = _prep_weights(dict(
        norm1_g=norm1_g, w_in=w_in, conv_w=conv_w, conv_b=conv_b, conv_ln_g=conv_ln_g,
        conv_ln_b=conv_ln_b, sg_ln_g=sg_ln_g, sg_ln_b=sg_ln_b, sg_w=sg_w, sg_b=sg_b,
        q_norm_g=q_norm_g, w_uq=w_uq, kv_norm_g=kv_norm_g, w_ukv=w_ukv, w_out=w_out,
        norm2_g=norm2_g, ffn_w_up=ffn_w_up, ffn_conv_w=ffn_conv_w, ffn_conv_b=ffn_conv_b,
        ffn_w_down=ffn_w_down))
    bp, lp_, _ = x_prompt.shape
    bs, ls, _ = x_sample.shape

    cvec = jnp.concatenate([c_ctx[None], c, jnp.zeros((8 - 1 - bs, D_MODEL), F32)], axis=0)
    mod_all = _ada_mod(cvec, w_ada, b_ada).reshape(DEPTH, 8, 6, D_MODEL)
    rope = _rope_tables(ls)
    fg = final_g[None]

    xp = x_prompt.reshape(bp * lp_, D_MODEL)
    xs = x_sample.reshape(bs * ls, D_MODEL)
    ckv_out, kr_out = [], []
    for l in range(DEPTH):
        last = l == DEPTH - 1
        mod_ctx = mod_all[l, 0:1]
        mod_lat = mod_all[l, 1:1 + bs]

        a, b, q, k, v, ckv_n, kr = _in_proj(xp, mod_ctx, w, l, seq_len=lp_, rope=None, emit_cache=True, tm=TM_IN)
        ckv_out.append(ckv_n.reshape(bp, lp_, KV_RANK))
        kr_out.append(kr.reshape(bp, lp_, ROPE_DIM))
        o = _attention(q, k, v, None, batch=bp, seq_len=lp_, tq=lp_, kc=lp_, heads_per_step=N_HEADS)
        xp = _out_proj(xp, mod_ctx, a, b, o, w, l, seq_len=lp_, tm=TM_OUT_CTX)
        xp = _ffn(xp, mod_ctx, w, l, fg, seq_len=lp_, tm=TM_FFN, tf=TF_FFN, final_norm=last)

        ctx = _ctx_kv(cache_ckv, cache_krope, w, l)
        a, b, q, k, v = _in_proj(xs, mod_lat, w, l, seq_len=ls, rope=rope, emit_cache=False, tm=TM_IN)
        o = _attention(q, k, v, ctx, batch=bs, seq_len=ls, tq=TQ_LAT, kc=KC_LAT, heads_per_step=1)
        xs = _out_proj(xs, mod_lat, a, b, o, w, l, seq_len=ls, tm=TM_OUT_LAT)
        xs = _ffn(xs, mod_lat, w, l, fg, seq_len=ls, tm=TM_FFN, tf=TF_FFN, final_norm=last)

    return (xp.reshape(bp, lp_, D_MODEL), xs.reshape(bs, ls, D_MODEL),
            jnp.stack(ckv_out, axis=1), jnp.stack(kr_out, axis=1))
```

```python
import functools

import jax
import jax.numpy as jnp
from jax import lax
from jax.experimental import pallas as pl
from jax.experimental.pallas import tpu as pltpu

D_MODEL = 2048
DEPTH = 2
GRID_W = 64
D_A = 512
G_A = 4
CONV_K = 31
D_B = 512
G_B = 4
CHUNK = 128
NOPE_DIM = 128
ROPE_DIM = 64
V_DIM = 128
N_HEADS = 8
Q_RANK = 512
KV_RANK = 256
D_FF = 5632
FFN_K = 3
ROPE_THETA = 10000.0
EPS = 1e-6
ATTN_SCALE = (NOPE_DIM + ROPE_DIM) ** -0.5

LANE = 128
SUBLANE = 8
QK_DIM = 2 * LANE
VO_DIM = 2 * LANE
Q_SCALE = ATTN_SCALE * 1.4426950408889634
HALO = 16
VMEM_LIMIT = 60 * 1024 * 1024

C_ZA = 0
C_ZB = 2 * D_A
C_ZQ = C_ZB + 2 * D_B
C_ZKV = C_ZQ + Q_RANK
C_ZKR = C_ZKV + KV_RANK
N_IN_EXT = C_ZKR + 2 * LANE
HW = N_HEADS * LANE

F32 = jnp.float32
BF16 = jnp.bfloat16


def _dot(a, b):
    return jnp.dot(a, b, preferred_element_type=F32)


def _rmsnorm(x, g):
    return x * lax.rsqrt(jnp.mean(x * x, axis=-1, keepdims=True) + EPS) * g


def _group_ln(x, g, b, groups):
    outs = []
    for i in range(groups):
        xg = x[:, i * LANE:(i + 1) * LANE]
        mu = jnp.mean(xg, axis=-1, keepdims=True)
        xc = xg - mu
        var = jnp.mean(xc * xc, axis=-1, keepdims=True)
        outs.append(xc * lax.rsqrt(var + EPS))
    return jnp.concatenate(outs, axis=-1) * g + b


def _sigmoid(x):
    return 1.0 / (1.0 + jnp.exp(-x))


def _layer_spec(shape, l):
    nd = len(shape)
    return pl.BlockSpec((None,) + tuple(shape), lambda *_: (l,) + (0,) * nd, pipeline_mode=pl.Buffered(1))


def _params(*sem):
    return pltpu.CompilerParams(dimension_semantics=sem, vmem_limit_bytes=VMEM_LIMIT)


def _ada_kernel(c_ref, w_ref, b_ref, o_ref):
    c = c_ref[...]
    s = (c * _sigmoid(c)).astype(BF16)
    o_ref[0] = _dot(s, w_ref[0].astype(BF16)) + b_ref[0]


def _ada_mod(cvec, w_ada, b_ada):
    tn = 1024
    n = 6 * D_MODEL
    return pl.pallas_call(
        _ada_kernel,
        out_shape=jax.ShapeDtypeStruct((DEPTH, 8, n), F32),
        grid=(DEPTH, n // tn),
        in_specs=[
            pl.BlockSpec((8, D_MODEL), lambda l, j: (0, 0)),
            pl.BlockSpec((1, D_MODEL, tn), lambda l, j: (l, 0, j)),
            pl.BlockSpec((1, 1, tn), lambda l, j: (l, 0, j)),
        ],
        out_specs=pl.BlockSpec((1, 8, tn), lambda l, j: (l, 0, j)),
        compiler_params=_params("arbitrary", "arbitrary"),
        name="ada_mod",
    )(cvec, w_ada, b_ada.reshape(DEPTH, 1, n))


def _in_proj_kernel(*refs, tm, use_rope, emit_cache):
    (x_ref, mod_ref, g1_ref, win_ref, sgg_ref, sgb_ref, sgw_ref, sgbias_ref,
     qg_ref, wuq_ref, kvg_ref, wukv_ref) = refs[:12]
    refs = refs[12:]
    if use_rope:
        cos_ref, sin_ref = refs[:2]
        refs = refs[2:]
    a_ref, b_ref, q_ref, k_ref, v_ref = refs[:5]
    refs = refs[5:]
    if emit_cache:
        ckv_ref, kr_ref = refs

    mod = mod_ref[0]
    h = _rmsnorm(x_ref[...], g1_ref[...]) * (1.0 + mod[1:2]) + mod[0:1]
    h = h.astype(BF16)

    za = _dot(h, win_ref[:, C_ZA:C_ZA + 2 * D_A])
    a_ref[...] = (za[:, :D_A] * _sigmoid(za[:, D_A:])).astype(BF16)

    zb = _dot(h, win_ref[:, C_ZB:C_ZB + 2 * D_B])
    zb = 0.5 * zb * (1.0 + lax.erf(zb * (2.0 ** -0.5)))
    u = zb[:, :D_B]
    vn = _group_ln(zb[:, D_B:], sgg_ref[...], sgb_ref[...], G_B).astype(BF16)
    for c in range(tm // CHUNK):
        rows = slice(c * CHUNK, (c + 1) * CHUNK)
        for g in range(G_B):
            cols = slice(g * LANE, (g + 1) * LANE)
            sv = _dot(sgw_ref[g], vn[rows, cols]) + sgbias_ref[g]
            b_ref[rows, cols] = (u[rows, cols] * sv).astype(BF16)

    zq = _dot(h, win_ref[:, C_ZQ:C_ZQ + Q_RANK])
    qn = _rmsnorm(zq, qg_ref[...]).astype(BF16)
    q = _dot(qn, wuq_ref[...])
    if use_rope:
        cos = cos_ref[...]
        sin = sin_ref[...]
    else:
        rope_lanes = lax.broadcasted_iota(jnp.int32, (1, LANE), 1) < ROPE_DIM
    for hd in range(N_HEADS):
        cols = slice(hd * LANE, (hd + 1) * LANE)
        q_ref[hd, :, 0:LANE] = (q[:, cols] * Q_SCALE).astype(BF16)
        qr = q[:, HW:2 * HW][:, cols]
        if use_rope:
            qr = qr * cos + pltpu.roll(qr, ROPE_DIM, axis=1) * sin
        else:
            qr = jnp.where(rope_lanes, qr, 0.0)
        q_ref[hd, :, LANE:QK_DIM] = (qr * Q_SCALE).astype(BF16)

    nkv = KV_RANK + (2 * LANE if use_rope else LANE)
    zkv = _dot(h, win_ref[:, C_ZKV:C_ZKV + nkv])
    ckv_n = _rmsnorm(zkv[:, :KV_RANK], kvg_ref[...])
    kr = zkv[:, KV_RANK:KV_RANK + LANE]
    if emit_cache:
        ckv_ref[...] = ckv_n
        kr_ref[...] = kr[:, :ROPE_DIM]
    if use_rope:
        kr = kr * cos + zkv[:, KV_RANK + LANE:KV_RANK + 2 * LANE] * sin
    kr = kr.astype(BF16)
    kv = _dot(ckv_n.astype(BF16), wukv_ref[...])
    ones = jnp.ones((tm, VO_DIM - V_DIM), BF16)
    for hd in range(N_HEADS):
        cols = slice(hd * LANE, (hd + 1) * LANE)
        k_ref[hd, :, 0:LANE] = kv[:, cols].astype(BF16)
        k_ref[hd, :, LANE:QK_DIM] = kr
        v_ref[hd, :, 0:V_DIM] = kv[:, HW:2 * HW][:, cols].astype(BF16)
        v_ref[hd, :, V_DIM:VO_DIM] = ones


def _in_proj(x, mod, w, l, *, seq_len, rope, emit_cache, tm):
    t = x.shape[0]
    use_rope = rope is not None
    tiles_per_seq = max(seq_len // tm, 1)
    n_mod = mod.shape[0]

    def mod_idx(i):
        return (jnp.minimum(i * tm // seq_len, n_mod - 1), 0, 0)

    in_specs = [
        pl.BlockSpec((tm, D_MODEL), lambda i: (i, 0)),
        pl.BlockSpec((1, 6, D_MODEL), mod_idx),
        _layer_spec((1, D_MODEL), l),
        _layer_spec((D_MODEL, N_IN_EXT), l),
        _layer_spec((1, D_B), l), _layer_spec((1, D_B), l),
        _layer_spec((G_B, CHUNK, CHUNK), l), _layer_spec((G_B, CHUNK, LANE), l),
        _layer_spec((1, Q_RANK), l),
        _layer_spec((Q_RANK, 2 * HW), l),
        _layer_spec((1, KV_RANK), l),
        _layer_spec((KV_RANK, 2 * HW), l),
    ]
    args = [x, mod, w["norm1_g"], w["w_in"], w["sg_ln_g"], w["sg_ln_b"], w["sg_w"], w["sg_bias"],
            w["q_norm_g"], w["w_uq"], w["kv_norm_g"], w["w_ukv"]]
    if use_rope:
        in_specs += [pl.BlockSpec((tm, LANE), lambda i: (i % tiles_per_seq, 0))] * 2
        args += list(rope)
    out_shape = [
        jax.ShapeDtypeStruct((t, D_A), BF16),
        jax.ShapeDtypeStruct((t, D_B), BF16),
        jax.ShapeDtypeStruct((N_HEADS, t, QK_DIM), BF16),
        jax.ShapeDtypeStruct((N_HEADS, t, QK_DIM), BF16),
        jax.ShapeDtypeStruct((N_HEADS, t, VO_DIM), BF16),
    ]
    out_specs = [
        pl.BlockSpec((tm, D_A), lambda i: (i, 0)),
        pl.BlockSpec((tm, D_B), lambda i: (i, 0)),
        pl.BlockSpec((N_HEADS, tm, QK_DIM), lambda i: (0, i, 0)),
        pl.BlockSpec((N_HEADS, tm, QK_DIM), lambda i: (0, i, 0)),
        pl.BlockSpec((N_HEADS, tm, VO_DIM), lambda i: (0, i, 0)),
    ]
    if emit_cache:
        out_shape += [jax.ShapeDtypeStruct((t, KV_RANK), F32), jax.ShapeDtypeStruct((t, ROPE_DIM), F32)]
        out_specs += [pl.BlockSpec((tm, KV_RANK), lambda i: (i, 0)),
                      pl.BlockSpec((tm, ROPE_DIM), lambda i: (i, 0))]
    return pl.pallas_call(
        functools.partial(_in_proj_kernel, tm=tm, use_rope=use_rope, emit_cache=emit_cache),
        out_shape=out_shape,
        grid=(t // tm,),
        in_specs=in_specs,
        out_specs=out_specs,
        compiler_params=_params("parallel"),
        name="in_proj",
    )(*args)


def _ctx_kv_kernel(ckv_ref, kr_ref, wukv_ref, k_ref, v_ref):
    kv = _dot(ckv_ref[...].astype(BF16), wukv_ref[...])
    kr = kr_ref[...].astype(BF16)
    zeros = jnp.zeros((kr.shape[0], LANE - ROPE_DIM), BF16)
    ones = jnp.ones((kr.shape[0], VO_DIM - V_DIM), BF16)
    for hd in range(N_HEADS):
        cols = slice(hd * LANE, (hd + 1) * LANE)
        k_ref[hd, :, 0:LANE] = kv[:, cols].astype(BF16)
        k_ref[hd, :, LANE:LANE + ROPE_DIM] = kr
        k_ref[hd, :, LANE + ROPE_DIM:QK_DIM] = zeros
        v_ref[hd, :, 0:V_DIM] = kv[:, HW:2 * HW][:, cols].astype(BF16)
        v_ref[hd, :, V_DIM:VO_DIM] = ones


def _ctx_kv(cache_ckv, cache_krope, w, l):
    bsz, _, past, _ = cache_ckv.shape
    return pl.pallas_call(
        _ctx_kv_kernel,
        out_shape=[jax.ShapeDtypeStruct((N_HEADS, bsz * past, QK_DIM), BF16),
                   jax.ShapeDtypeStruct((N_HEADS, bsz * past, VO_DIM), BF16)],
        grid=(bsz,),
        in_specs=[pl.BlockSpec((None, None, past, KV_RANK), lambda b: (b, l, 0, 0)),
                  pl.BlockSpec((None, None, past, ROPE_DIM), lambda b: (b, l, 0, 0)),
                  _layer_spec((KV_RANK, 2 * HW), l)],
        out_specs=[pl.BlockSpec((N_HEADS, past, QK_DIM), lambda b: (0, b, 0)),
                   pl.BlockSpec((N_HEADS, past, VO_DIM), lambda b: (0, b, 0))],
        compiler_params=_params("parallel"),
        name="ctx_kv",
    )(cache_ckv, cache_krope, w["w_ukv"])


def _attn_kernel(*refs, heads, has_ctx, kc):
    if has_ctx:
        q_ref, kl_ref, vl_ref, kx_ref, vx_ref, o_ref = refs
    else:
        q_ref, kl_ref, vl_ref, o_ref = refs
    for hd in range(heads):
        q = q_ref[hd]
        chunks = []
        if has_ctx:
            chunks.append((kx_ref, vx_ref, 0, kx_ref.shape[1]))
        n_keys = kl_ref.shape[1]
        for s0 in range(0, n_keys, kc):
            chunks.append((kl_ref, vl_ref, s0, min(kc, n_keys - s0)))
        m = pv = None
        for (k_r, v_r, s0, n) in chunks:
            k = k_r[hd, s0:s0 + n, :]
            v = v_r[hd, s0:s0 + n, :]
            s = lax.dot_general(q, k, (((1,), (1,)), ((), ())), preferred_element_type=F32)
            m_c = jnp.max(s, axis=-1, keepdims=True)
            if m is None:
                m = m_c
                pv = _dot(jnp.exp2(s - m).astype(BF16), v)
            else:
                m_new = jnp.maximum(m, m_c)
                pv = jnp.exp2(m - m_new) * pv + _dot(jnp.exp2(s - m_new).astype(BF16), v)
                m = m_new
        o_ref[:, hd * V_DIM:(hd + 1) * V_DIM] = (pv[:, :V_DIM] / pv[:, V_DIM:]).astype(o_ref.dtype)


def _attention(q, k, v, ctx, *, batch, seq_len, tq, kc, heads_per_step):
    t = q.shape[1]
    nq = seq_len // tq
    hb = heads_per_step
    has_ctx = ctx is not None
    in_specs = [
        pl.BlockSpec((hb, tq, QK_DIM), lambda b, h, i: (h, b * nq + i, 0)),
        pl.BlockSpec((hb, seq_len, QK_DIM), lambda b, h, i: (h, b, 0)),
        pl.BlockSpec((hb, seq_len, VO_DIM), lambda b, h, i: (h, b, 0)),
    ]
    args = [q, k, v]
    if has_ctx:
        past = ctx[0].shape[1] // batch
        in_specs += [pl.BlockSpec((hb, past, QK_DIM), lambda b, h, i: (h, b, 0)),
                     pl.BlockSpec((hb, past, VO_DIM), lambda b, h, i: (h, b, 0))]
        args += list(ctx)
    return pl.pallas_call(
        functools.partial(_attn_kernel, heads=hb, has_ctx=has_ctx, kc=kc),
        out_shape=jax.ShapeDtypeStruct((t, N_HEADS * V_DIM), BF16),
        grid=(batch, N_HEADS // hb, nq),
        in_specs=in_specs,
        out_specs=pl.BlockSpec((tq, hb * V_DIM), lambda b, h, i: (b * nq + i, h)),
        compiler_params=_params("parallel", "parallel", "arbitrary"),
        name="attention",
    )(*args)


CONV_ROWS = 64


def _conv_copy_rows(seg):
    return seg + (HALO + CONV_K // 2) // SUBLANE * SUBLANE


def _out_proj_kernel(*refs, tm, seg, halo, tiles_per_seq):
    x_ref, mod_ref, a_ref = refs[:3]
    refs = refs[3:]
    if halo:
        ap_ref, an_ref = refs[:2]
        refs = refs[2:]
    b_ref, o_ref, cw_ref, cb_ref, lg_ref, lb_ref, wout_ref, y_ref, abuf, sh, aout, cat = refs

    cw = cw_ref[...]
    ext = _conv_copy_rows(seg)
    for sgm in range(tm // seg):
        r0 = sgm * seg
        if halo:
            ti = pl.program_id(0) % tiles_per_seq
            abuf[0:HALO, :] = ap_ref[...].astype(F32) * jnp.where(ti > 0, 1.0, 0.0)
            abuf[HALO + seg:, :] = an_ref[...].astype(F32) * jnp.where(ti < tiles_per_seq - 1, 1.0, 0.0)
        else:
            abuf[0:HALO, :] = jnp.zeros((HALO, D_A), F32)
            abuf[HALO + seg:, :] = jnp.zeros((HALO, D_A), F32)
        abuf[HALO:HALO + seg, :] = a_ref[r0:r0 + seg, :].astype(F32)
        for r in range(1, SUBLANE):
            sh[r - 1, :, :] = abuf[r:r + ext, :]
        for t0 in range(0, seg, CONV_ROWS):
            acc = jnp.zeros((CONV_ROWS, D_A), F32) + cb_ref[...]
            for kk in range(CONV_K):
                q8, r = divmod(HALO - CONV_K // 2 + kk, SUBLANE)
                rows = slice(t0 + q8 * SUBLANE, t0 + q8 * SUBLANE + CONV_ROWS)
                src = abuf[rows, :] if r == 0 else sh[r - 1, rows, :]
                acc = acc + src * cw[kk:kk + 1, :]
            an = _group_ln(acc, lg_ref[...], lb_ref[...], G_A)
            aout[r0 + t0:r0 + t0 + CONV_ROWS, :] = (an * _sigmoid(an)).astype(BF16)

    gate = mod_ref[0][2:3]
    cat[:, 0:D_B] = b_ref[...]
    cat[:, D_B:] = o_ref[...]
    y_ref[...] = x_ref[...] + gate * _dot(cat[...], wout_ref[D_A:, :])
    y_ref[...] += gate * _dot(aout[...], wout_ref[0:D_A, :])


def _out_proj(x, mod, a, b, o, w, l, *, seq_len, tm):
    t = x.shape[0]
    halo = tm < seq_len
    assert (seq_len % tm == 0) if halo else (tm % seq_len == 0)
    seg = tm if halo else seq_len
    tiles_per_seq = max(seq_len // tm, 1)
    hb = tm // HALO
    n_hb = t // HALO
    n_mod = mod.shape[0]

    def mod_idx(i):
        return (jnp.minimum(i * tm // seq_len, n_mod - 1), 0, 0)

    d_mix = D_A + D_B + N_HEADS * V_DIM
    in_specs = [
        pl.BlockSpec((tm, D_MODEL), lambda i: (i, 0)),
        pl.BlockSpec((1, 6, D_MODEL), mod_idx),
        pl.BlockSpec((tm, D_A), lambda i: (i, 0)),
    ]
    args = [x, mod, a]
    if halo:
        in_specs += [pl.BlockSpec((HALO, D_A), lambda i: (jnp.maximum(i * hb - 1, 0), 0)),
                     pl.BlockSpec((HALO, D_A), lambda i: (jnp.minimum((i + 1) * hb, n_hb - 1), 0))]
        args += [a, a]
    in_specs += [
        pl.BlockSpec((tm, D_B), lambda i: (i, 0)),
        pl.BlockSpec((tm, N_HEADS * V_DIM), lambda i: (i, 0)),
        _layer_spec((CONV_K, D_A), l), _layer_spec((1, D_A), l), _layer_spec((1, D_A), l),
        _layer_spec((1, D_A), l),
        _layer_spec((d_mix, D_MODEL), l),
    ]
    args += [b, o, w["conv_w"], w["conv_b"], w["conv_ln_g"], w["conv_ln_b"], w["w_out"]]
    return pl.pallas_call(
        functools.partial(_out_proj_kernel, tm=tm, seg=seg, halo=halo, tiles_per_seq=tiles_per_seq),
        out_shape=jax.ShapeDtypeStruct((t, D_MODEL), F32),
        grid=(t // tm,),
        in_specs=in_specs,
        out_specs=pl.BlockSpec((tm, D_MODEL), lambda i: (i, 0)),
        scratch_shapes=[pltpu.VMEM((seg + 2 * HALO, D_A), F32),
                        pltpu.VMEM((SUBLANE - 1, _conv_copy_rows(seg), D_A), F32),
                        pltpu.VMEM((tm, D_A), BF16),
                        pltpu.VMEM((tm, d_mix - D_A), BF16)],
        compiler_params=_params("parallel"),
        name="out_proj",
    )(*args)


def _ffn_kernel(*refs, tm, seq_len, halo, final_norm):
    x_ref = refs[0]
    refs = refs[1:]
    if halo:
        xp_ref, xn_ref = refs[:2]
        refs = refs[2:]
    (mod_ref, g2_ref, wg_ref, wv_ref, cwg_ref, cwv_ref, cbg_ref, cbv_ref, wdn_ref, fg_ref,
     y_ref, hs, zg, zv) = refs
    i = pl.program_id(0)
    j = pl.program_id(1)
    nj = pl.num_programs(1)
    mod = mod_ref[0]
    pad = HALO if halo else SUBLANE
    tf = zg.shape[1]

    @pl.when(j == 0)
    def _():
        g2 = g2_ref[...]
        scale = 1.0 + mod[4:5]
        shift = mod[3:4]
        x = x_ref[...]
        if halo:
            tiles_per_seq = seq_len // tm
            ti = i % tiles_per_seq
            prev_ok = jnp.where(ti > 0, 1.0, 0.0)
            next_ok = jnp.where(ti < tiles_per_seq - 1, 1.0, 0.0)
            hs[0:HALO, :] = ((_rmsnorm(xp_ref[...], g2) * scale + shift) * prev_ok).astype(BF16)
            hs[HALO:HALO + tm, :] = (_rmsnorm(x, g2) * scale + shift).astype(BF16)
            hs[HALO + tm:, :] = ((_rmsnorm(xn_ref[...], g2) * scale + shift) * next_ok).astype(BF16)
        else:
            hs[...] = (_rmsnorm(x, g2) * scale + shift).astype(BF16)
            zeros = jnp.zeros((pad, tf), F32)
            for z in (zg, zv):
                z[0:pad, :] = zeros
                z[pad + tm:, :] = zeros
        y_ref[...] = jnp.zeros_like(y_ref)

    if halo:
        zg[...] = _dot(hs[...], wg_ref[...])
        zv[...] = _dot(hs[...], wv_ref[...])
    else:
        zg[pad:pad + tm, :] = _dot(hs[...], wg_ref[...])
        zv[pad:pad + tm, :] = _dot(hs[...], wv_ref[...])
        pos = lax.rem(lax.broadcasted_iota(jnp.int32, (tm, 1), 0), seq_len)
        not_first = pos != 0
        not_last = pos != seq_len - 1

    def conv(z, cw_ref, cb_ref):
        cw = cw_ref[...]
        zp = z[pad - 1:pad - 1 + tm, :]
        zn = z[pad + 1:pad + 1 + tm, :]
        if not halo:
            zp = jnp.where(not_first, zp, 0.0)
            zn = jnp.where(not_last, zn, 0.0)
        return cb_ref[...] + zp * cw[0:1, :] + z[pad:pad + tm, :] * cw[1:2, :] + zn * cw[2:3, :]

    g = conv(zg, cwg_ref, cbg_ref)
    act = (g * _sigmoid(g) * conv(zv, cwv_ref, cbv_ref)).astype(BF16)
    y_ref[...] += _dot(act, wdn_ref[...])

    @pl.when(j == nj - 1)
    def _():
        y = x_ref[...] + mod[5:6] * y_ref[...]
        if final_norm:
            y = _rmsnorm(y, fg_ref[...])
        y_ref[...] = y


def _ffn(x, mod, w, l, final_g, *, seq_len, tm, tf, final_norm):
    t = x.shape[0]
    halo = tm < seq_len
    assert (seq_len % tm == 0) if halo else (tm % seq_len == 0)
    nf = D_FF // tf
    hb = tm // HALO
    n_hb = t // HALO
    n_mod = mod.shape[0]
    pad = HALO if halo else SUBLANE

    def mod_idx(i, j):
        return (jnp.minimum(i * tm // seq_len, n_mod - 1), 0, 0)

    def lspec(shape, idx):
        return pl.BlockSpec((None,) + shape, lambda i, j: (l,) + idx(i, j))

    in_specs = [pl.BlockSpec((tm, D_MODEL), lambda i, j: (i, 0), pipeline_mode=pl.Buffered(1))]
    args = [x]
    if halo:
        in_specs += [
            pl.BlockSpec((HALO, D_MODEL), lambda i, j: (jnp.maximum(i * hb - 1, 0), 0)),
            pl.BlockSpec((HALO, D_MODEL), lambda i, j: (jnp.minimum((i + 1) * hb, n_hb - 1), 0)),
        ]
        args += [x, x]
    in_specs += [
        pl.BlockSpec((1, 6, D_MODEL), mod_idx),
        _layer_spec((1, D_MODEL), l),
        lspec((D_MODEL, tf), lambda i, j: (0, j)),
        lspec((D_MODEL, tf), lambda i, j: (0, nf + j)),
        lspec((FFN_K, tf), lambda i, j: (0, j)),
        lspec((FFN_K, tf), lambda i, j: (0, nf + j)),
        lspec((1, tf), lambda i, j: (0, j)),
        lspec((1, tf), lambda i, j: (0, nf + j)),
        lspec((tf, D_MODEL), lambda i, j: (j, 0)),
        pl.BlockSpec((1, D_MODEL), lambda i, j: (0, 0)),
    ]
    args += [mod, w["norm2_g"], w["w_up"], w["w_up"], w["ffn_conv_w"], w["ffn_conv_w"],
             w["ffn_conv_b"], w["ffn_conv_b"], w["w_down"], final_g]
    h_rows = tm + 2 * HALO if halo else tm
    return pl.pallas_call(
        functools.partial(_ffn_kernel, tm=tm, seq_len=seq_len, halo=halo, final_norm=final_norm),
        out_shape=jax.ShapeDtypeStruct((t, D_MODEL), F32),
        grid=(t // tm, nf),
        in_specs=in_specs,
        out_specs=pl.BlockSpec((tm, D_MODEL), lambda i, j: (i, 0)),
        scratch_shapes=[pltpu.VMEM((h_rows, D_MODEL), BF16),
                        pltpu.VMEM((tm + 2 * pad, tf), F32),
                        pltpu.VMEM((tm + 2 * pad, tf), F32)],
        compiler_params=_params("parallel", "arbitrary"),
        name="conv_ffn",
    )(*args)


def _rope_tables(length):
    pos = jnp.arange(length)
    r = (pos // GRID_W).astype(F32)
    col = (pos % GRID_W).astype(F32)
    half = ROPE_DIM // 4
    freq = 1.0 / (ROPE_THETA ** (jnp.arange(half, dtype=F32) / half))
    ar = r[:, None] * freq
    ac = col[:, None] * freq
    zeros = jnp.zeros((length, LANE - ROPE_DIM), F32)
    cos = jnp.concatenate([jnp.cos(ar), jnp.cos(ar), jnp.cos(ac), jnp.cos(ac), zeros], axis=1)
    sin = jnp.concatenate([-jnp.sin(ar), jnp.sin(ar), -jnp.sin(ac), jnp.sin(ac), zeros], axis=1)
    return cos, sin


def _swap_rope_cols(w):
    q = ROPE_DIM // 4
    return jnp.concatenate([w[..., q:2 * q], w[..., 0:q], w[..., 3 * q:4 * q], w[..., 2 * q:3 * q]], axis=-1)


def _prep_weights(p):
    w_in = p["w_in"]
    zkr_w = w_in[:, :, C_ZKR:C_ZKR + ROPE_DIM]
    zpad = jnp.zeros((DEPTH, D_MODEL, LANE - ROPE_DIM), F32)
    w_in_ext = jnp.concatenate([w_in[:, :, :C_ZKR], zkr_w, zpad, _swap_rope_cols(zkr_w), zpad], axis=-1)

    wuq = p["w_uq"].reshape(DEPTH, Q_RANK, N_HEADS, NOPE_DIM + ROPE_DIM)
    nope = wuq[..., :NOPE_DIM].reshape(DEPTH, Q_RANK, HW)
    rope = wuq[..., NOPE_DIM:]
    rope_p = jnp.concatenate([rope, _swap_rope_cols(rope)], axis=-1).reshape(DEPTH, Q_RANK, HW)

    wukv = p["w_ukv"].reshape(DEPTH, KV_RANK, N_HEADS, NOPE_DIM + V_DIM)
    wukv = jnp.concatenate([wukv[..., :NOPE_DIM].reshape(DEPTH, KV_RANK, HW),
                            wukv[..., NOPE_DIM:].reshape(DEPTH, KV_RANK, HW)], axis=-1)

    def vec(name):
        return p[name][:, None, :]

    return {
        "norm1_g": vec("norm1_g"),
        "w_in": w_in_ext.astype(BF16),
        "sg_ln_g": vec("sg_ln_g"), "sg_ln_b": vec("sg_ln_b"),
        "sg_w": p["sg_w"].astype(BF16),
        "sg_bias": jnp.broadcast_to(p["sg_b"][..., None], (DEPTH, G_B, CHUNK, LANE)),
        "q_norm_g": vec("q_norm_g"),
        "w_uq": jnp.concatenate([nope, rope_p], axis=-1).astype(BF16),
        "kv_norm_g": vec("kv_norm_g"),
        "w_ukv": wukv.astype(BF16),
        "conv_w": p["conv_w"], "conv_b": vec("conv_b"),
        "conv_ln_g": vec("conv_ln_g"), "conv_ln_b": vec("conv_ln_b"),
        "w_out": p["w_out"].astype(BF16),
        "norm2_g": vec("norm2_g"),
        "w_up": p["ffn_w_up"].astype(BF16),
        "ffn_conv_w": p["ffn_conv_w"], "ffn_conv_b": vec("ffn_conv_b"),
        "w_down": p["ffn_w_down"].astype(BF16),
    }


TM_IN = 512
TM_OUT = 512
TM_FFN = 1024
TF_FFN = 512
TQ_LAT = 512
KC_LAT = 1024


def kernel(x_prompt, x_sample, cache_ckv, cache_krope, c, c_ctx, norm1_g, w_ada, b_ada, w_in, conv_w, conv_b, conv_ln_g, conv_ln_b, sg_ln_g, sg_ln_b, sg_w, sg_b, q_norm_g, w_uq, kv_norm_g, w_ukv, w_out, norm2_g, ffn_w_up, ffn_conv_w, ffn_conv_b, ffn_w_down, final_g):
    w = _prep_weights(dict(
        norm1_g=norm1_g, w_in=w_in, conv_w=conv_w, conv_b=conv_b, conv_ln_g=conv_ln_g,
        conv_ln_b=conv_ln_b, sg_ln_g=sg_ln_g, sg_ln_b=sg_ln_b, sg_w=sg_w, sg_b=sg_b,
        q_norm_g=q_norm_g, w_uq=w_uq, kv_norm_g=kv_norm_g, w_ukv=w_ukv, w_out=w_out,
        norm2_g=norm2_g, ffn_w_up=ffn_w_up, ffn_conv_w=ffn_conv_w, ffn_conv_b=ffn_conv_b,
        ffn_w_down=ffn_w_down))
    bp, lp_, _ = x_prompt.shape
    bs, ls, _ = x_sample.shape

    cvec = jnp.concatenate([c_ctx[None], c, jnp.zeros((8 - 1 - bs, D_MODEL), F32)], axis=0)
    mod_all = _ada_mod(cvec, w_ada, b_ada).reshape(DEPTH, 8, 6, D_MODEL)
    rope = _rope_tables(ls)
    fg = final_g[None]

    xp = x_prompt.reshape(bp * lp_, D_MODEL)
    xs = x_sample.reshape(bs * ls, D_MODEL)
    ckv_out, kr_out = [], []
    for l in range(DEPTH):
        last = l == DEPTH - 1
        mod_ctx = mod_all[l, 0:1]
        mod_lat = mod_all[l, 1:1 + bs]

        a, b, q, k, v, ckv_n, kr = _in_proj(xp, mod_ctx, w, l, seq_len=lp_, rope=None, emit_cache=True, tm=TM_IN)
        ckv_out.append(ckv_n.reshape(bp, lp_, KV_RANK))
        kr_out.append(kr.reshape(bp, lp_, ROPE_DIM))
        o = _attention(q, k, v, None, batch=bp, seq_len=lp_, tq=lp_, kc=lp_, heads_per_step=N_HEADS)
        xp = _out_proj(xp, mod_ctx, a, b, o, w, l, seq_len=lp_, tm=TM_OUT)
        xp = _ffn(xp, mod_ctx, w, l, fg, seq_len=lp_, tm=TM_FFN, tf=TF_FFN, final_norm=last)

        ctx = _ctx_kv(cache_ckv, cache_krope, w, l)
        a, b, q, k, v = _in_proj(xs, mod_lat, w, l, seq_len=ls, rope=rope, emit_cache=False, tm=TM_IN)
        o = _attention(q, k, v, ctx, batch=bs, seq_len=ls, tq=TQ_LAT, kc=KC_LAT, heads_per_step=4)
        xs = _out_proj(xs, mod_lat, a, b, o, w, l, seq_len=ls, tm=TM_OUT)
        xs = _ffn(xs, mod_lat, w, l, fg, seq_len=ls, tm=TM_FFN, tf=TF_FFN, final_norm=last)

    return (xp.reshape(bp, lp_, D_MODEL), xs.reshape(bs, ls, D_MODEL),
            jnp.stack(ckv_out, axis=1), jnp.stack(kr_out, axis=1))
```

```python
import functools

import jax
import jax.numpy as jnp
from jax import lax
from jax.experimental import pallas as pl
from jax.experimental.pallas import tpu as pltpu

D_MODEL = 2048
DEPTH = 2
GRID_W = 64
D_A = 512
G_A = 4
CONV_K = 31
D_B = 512
G_B = 4
CHUNK = 128
NOPE_DIM = 128
ROPE_DIM = 64
V_DIM = 128
N_HEADS = 8
Q_RANK = 512
KV_RANK = 256
D_FF = 5632
FFN_K = 3
ROPE_THETA = 10000.0
EPS = 1e-6
ATTN_SCALE = (NOPE_DIM + ROPE_DIM) ** -0.5

LANE = 128
SUBLANE = 8
QK_DIM = 2 * LANE
VO_DIM = 2 * LANE
Q_SCALE = ATTN_SCALE * 1.4426950408889634
HALO = 16
VMEM_LIMIT = 62 * 1024 * 1024

C_ZA = 0
C_ZB = 2 * D_A
C_ZQ = C_ZB + 2 * D_B
C_ZKV = C_ZQ + Q_RANK
C_ZKR = C_ZKV + KV_RANK
N_IN_EXT = C_ZKR + 2 * LANE
HW = N_HEADS * LANE

F32 = jnp.float32
BF16 = jnp.bfloat16


def _dot(a, b):
    return jnp.dot(a, b, preferred_element_type=F32)


def _rmsnorm(x, g):
    return x * lax.rsqrt(jnp.mean(x * x, axis=-1, keepdims=True) + EPS) * g


def _group_ln(x, g, b, groups):
    outs = []
    for i in range(groups):
        xg = x[:, i * LANE:(i + 1) * LANE]
        mu = jnp.mean(xg, axis=-1, keepdims=True)
        xc = xg - mu
        var = jnp.mean(xc * xc, axis=-1, keepdims=True)
        outs.append(xc * lax.rsqrt(var + EPS))
    return jnp.concatenate(outs, axis=-1) * g + b


def _sigmoid(x):
    return 1.0 / (1.0 + jnp.exp(-x))


def _layer_spec(shape, l):
    nd = len(shape)
    return pl.BlockSpec((None,) + tuple(shape), lambda *_: (l,) + (0,) * nd, pipeline_mode=pl.Buffered(1))


def _params(*sem):
    return pltpu.CompilerParams(dimension_semantics=sem, vmem_limit_bytes=VMEM_LIMIT)


def _ada_kernel(c_ref, w_ref, b_ref, o_ref):
    c = c_ref[...]
    s = (c * _sigmoid(c)).astype(BF16)
    o_ref[0] = _dot(s, w_ref[0].astype(BF16)) + b_ref[0]


def _ada_mod(cvec, w_ada, b_ada):
    tn = 1024
    n = 6 * D_MODEL
    return pl.pallas_call(
        _ada_kernel,
        out_shape=jax.ShapeDtypeStruct((DEPTH, 8, n), F32),
        grid=(DEPTH, n // tn),
        in_specs=[
            pl.BlockSpec((8, D_MODEL), lambda l, j: (0, 0)),
            pl.BlockSpec((1, D_MODEL, tn), lambda l, j: (l, 0, j)),
            pl.BlockSpec((1, 1, tn), lambda l, j: (l, 0, j)),
        ],
        out_specs=pl.BlockSpec((1, 8, tn), lambda l, j: (l, 0, j)),
        compiler_params=_params("arbitrary", "arbitrary"),
        name="ada_mod",
    )(cvec, w_ada, b_ada.reshape(DEPTH, 1, n))


def _in_proj_kernel(*refs, tm, use_rope, emit_cache):
    (x_ref, mod_ref, g1_ref, win_ref, sgg_ref, sgb_ref, sgw_ref, sgbias_ref,
     qg_ref, wuq_ref, kvg_ref, wukv_ref) = refs[:12]
    refs = refs[12:]
    if use_rope:
        cos_ref, sin_ref = refs[:2]
        refs = refs[2:]
    a_ref, b_ref, q_ref, k_ref, v_ref = refs[:5]
    refs = refs[5:]
    if emit_cache:
        ckv_ref, kr_ref = refs

    mod = mod_ref[0]
    h = _rmsnorm(x_ref[...], g1_ref[...]) * (1.0 + mod[1:2]) + mod[0:1]
    h = h.astype(BF16)

    za = _dot(h, win_ref[:, C_ZA:C_ZA + 2 * D_A])
    a_ref[...] = (za[:, :D_A] * _sigmoid(za[:, D_A:])).astype(BF16)

    zb = _dot(h, win_ref[:, C_ZB:C_ZB + 2 * D_B])
    zb = 0.5 * zb * (1.0 + lax.erf(zb * (2.0 ** -0.5)))
    u = zb[:, :D_B]
    vn = _group_ln(zb[:, D_B:], sgg_ref[...], sgb_ref[...], G_B).astype(BF16)
    for c in range(tm // CHUNK):
        rows = slice(c * CHUNK, (c + 1) * CHUNK)
        for g in range(G_B):
            cols = slice(g * LANE, (g + 1) * LANE)
            sv = _dot(sgw_ref[g], vn[rows, cols]) + sgbias_ref[g]
            b_ref[rows, cols] = (u[rows, cols] * sv).astype(BF16)

    zq = _dot(h, win_ref[:, C_ZQ:C_ZQ + Q_RANK])
    qn = _rmsnorm(zq, qg_ref[...]).astype(BF16)
    q = _dot(qn, wuq_ref[...])
    if use_rope:
        cos = cos_ref[...]
        sin = sin_ref[...]
    else:
        rope_lanes = lax.broadcasted_iota(jnp.int32, (1, LANE), 1) < ROPE_DIM
    for hd in range(N_HEADS):
        cols = slice(hd * LANE, (hd + 1) * LANE)
        q_ref[hd, :, 0:LANE] = (q[:, cols] * Q_SCALE).astype(BF16)
        qr = q[:, HW:2 * HW][:, cols]
        if use_rope:
            qr = qr * cos + pltpu.roll(qr, ROPE_DIM, axis=1) * sin
        else:
            qr = jnp.where(rope_lanes, qr, 0.0)
        q_ref[hd, :, LANE:QK_DIM] = (qr * Q_SCALE).astype(BF16)

    nkv = KV_RANK + (2 * LANE if use_rope else LANE)
    zkv = _dot(h, win_ref[:, C_ZKV:C_ZKV + nkv])
    ckv_n = _rmsnorm(zkv[:, :KV_RANK], kvg_ref[...])
    kr = zkv[:, KV_RANK:KV_RANK + LANE]
    if emit_cache:
        ckv_ref[...] = ckv_n
        kr_ref[...] = kr[:, :ROPE_DIM]
    if use_rope:
        kr = kr * cos + zkv[:, KV_RANK + LANE:KV_RANK + 2 * LANE] * sin
    kr = kr.astype(BF16)
    kv = _dot(ckv_n.astype(BF16), wukv_ref[...])
    ones = jnp.ones((tm, VO_DIM - V_DIM), BF16)
    for hd in range(N_HEADS):
        cols = slice(hd * LANE, (hd + 1) * LANE)
        k_ref[hd, :, 0:LANE] = kv[:, cols].astype(BF16)
        k_ref[hd, :, LANE:QK_DIM] = kr
        v_ref[hd, :, 0:V_DIM] = kv[:, HW:2 * HW][:, cols].astype(BF16)
        v_ref[hd, :, V_DIM:VO_DIM] = ones


def _in_proj(x, mod, w, l, *, seq_len, rope, emit_cache, tm):
    t = x.shape[0]
    use_rope = rope is not None
    tiles_per_seq = max(seq_len // tm, 1)
    n_mod = mod.shape[0]

    def mod_idx(i):
        return (jnp.minimum(i * tm // seq_len, n_mod - 1), 0, 0)

    in_specs = [
        pl.BlockSpec((tm, D_MODEL), lambda i: (i, 0)),
        pl.BlockSpec((1, 6, D_MODEL), mod_idx),
        _layer_spec((1, D_MODEL), l),
        _layer_spec((D_MODEL, N_IN_EXT), l),
        _layer_spec((1, D_B), l), _layer_spec((1, D_B), l),
        _layer_spec((G_B, CHUNK, CHUNK), l), _layer_spec((G_B, CHUNK, LANE), l),
        _layer_spec((1, Q_RANK), l),
        _layer_spec((Q_RANK, 2 * HW), l),
        _layer_spec((1, KV_RANK), l),
        _layer_spec((KV_RANK, 2 * HW), l),
    ]
    args = [x, mod, w["norm1_g"], w["w_in"], w["sg_ln_g"], w["sg_ln_b"], w["sg_w"], w["sg_bias"],
            w["q_norm_g"], w["w_uq"], w["kv_norm_g"], w["w_ukv"]]
    if use_rope:
        in_specs += [pl.BlockSpec((tm, LANE), lambda i: (i % tiles_per_seq, 0))] * 2
        args += list(rope)
    out_shape = [
        jax.ShapeDtypeStruct((t, D_A), BF16),
        jax.ShapeDtypeStruct((t, D_B), BF16),
        jax.ShapeDtypeStruct((N_HEADS, t, QK_DIM), BF16),
        jax.ShapeDtypeStruct((N_HEADS, t, QK_DIM), BF16),
        jax.ShapeDtypeStruct((N_HEADS, t, VO_DIM), BF16),
    ]
    out_specs = [
        pl.BlockSpec((tm, D_A), lambda i: (i, 0)),
        pl.BlockSpec((tm, D_B), lambda i: (i, 0)),
        pl.BlockSpec((N_HEADS, tm, QK_DIM), lambda i: (0, i, 0)),
        pl.BlockSpec((N_HEADS, tm, QK_DIM), lambda i: (0, i, 0)),
        pl.BlockSpec((N_HEADS, tm, VO_DIM), lambda i: (0, i, 0)),
    ]
    if emit_cache:
        out_shape += [jax.ShapeDtypeStruct((t, KV_RANK), F32), jax.ShapeDtypeStruct((t, ROPE_DIM), F32)]
        out_specs += [pl.BlockSpec((tm, KV_RANK), lambda i: (i, 0)),
                      pl.BlockSpec((tm, ROPE_DIM), lambda i: (i, 0))]
    return pl.pallas_call(
        functools.partial(_in_proj_kernel, tm=tm, use_rope=use_rope, emit_cache=emit_cache),
        out_shape=out_shape,
        grid=(t // tm,),
        in_specs=in_specs,
        out_specs=out_specs,
        compiler_params=_params("parallel"),
        name="in_proj",
    )(*args)


def _ctx_kv_kernel(ckv_ref, kr_ref, wukv_ref, k_ref, v_ref):
    kv = _dot(ckv_ref[...].astype(BF16), wukv_ref[...])
    kr = kr_ref[...].astype(BF16)
    zeros = jnp.zeros((kr.shape[0], LANE - ROPE_DIM), BF16)
    ones = jnp.ones((kr.shape[0], VO_DIM - V_DIM), BF16)
    for hd in range(N_HEADS):
        cols = slice(hd * LANE, (hd + 1) * LANE)
        k_ref[hd, :, 0:LANE] = kv[:, cols].astype(BF16)
        k_ref[hd, :, LANE:LANE + ROPE_DIM] = kr
        k_ref[hd, :, LANE + ROPE_DIM:QK_DIM] = zeros
        v_ref[hd, :, 0:V_DIM] = kv[:, HW:2 * HW][:, cols].astype(BF16)
        v_ref[hd, :, V_DIM:VO_DIM] = ones


def _ctx_kv(cache_ckv, cache_krope, w, l):
    bsz, _, past, _ = cache_ckv.shape
    return pl.pallas_call(
        _ctx_kv_kernel,
        out_shape=[jax.ShapeDtypeStruct((N_HEADS, bsz * past, QK_DIM), BF16),
                   jax.ShapeDtypeStruct((N_HEADS, bsz * past, VO_DIM), BF16)],
        grid=(bsz,),
        in_specs=[pl.BlockSpec((None, None, past, KV_RANK), lambda b: (b, l, 0, 0)),
                  pl.BlockSpec((None, None, past, ROPE_DIM), lambda b: (b, l, 0, 0)),
                  _layer_spec((KV_RANK, 2 * HW), l)],
        out_specs=[pl.BlockSpec((N_HEADS, past, QK_DIM), lambda b: (0, b, 0)),
                   pl.BlockSpec((N_HEADS, past, VO_DIM), lambda b: (0, b, 0))],
        compiler_params=_params("parallel"),
        name="ctx_kv",
    )(cache_ckv, cache_krope, w["w_ukv"])


def _attn_kernel(*refs, heads, has_ctx, kc):
    if has_ctx:
        q_ref, kl_ref, vl_ref, kx_ref, vx_ref, o_ref = refs
    else:
        q_ref, kl_ref, vl_ref, o_ref = refs
    for hd in range(heads):
        q = q_ref[hd]
        chunks = []
        if has_ctx:
            chunks.append((kx_ref, vx_ref, 0, kx_ref.shape[1]))
        n_keys = kl_ref.shape[1]
        for s0 in range(0, n_keys, kc):
            chunks.append((kl_ref, vl_ref, s0, min(kc, n_keys - s0)))
        m = pv = None
        for (k_r, v_r, s0, n) in chunks:
            k = k_r[hd, s0:s0 + n, :]
            v = v_r[hd, s0:s0 + n, :]
            s = lax.dot_general(q, k, (((1,), (1,)), ((), ())), preferred_element_type=F32)
            m_c = jnp.max(s, axis=-1, keepdims=True)
            if m is None:
                m = m_c
                pv = _dot(jnp.exp2(s - m).astype(BF16), v)
            else:
                m_new = jnp.maximum(m, m_c)
                pv = jnp.exp2(m - m_new) * pv + _dot(jnp.exp2(s - m_new).astype(BF16), v)
                m = m_new
        o_ref[:, hd * V_DIM:(hd + 1) * V_DIM] = (pv[:, :V_DIM] / pv[:, V_DIM:]).astype(o_ref.dtype)


def _attention(q, k, v, ctx, *, batch, seq_len, tq, kc, heads_per_step):
    t = q.shape[1]
    nq = seq_len // tq
    hb = heads_per_step
    has_ctx = ctx is not None
    in_specs = [
        pl.BlockSpec((hb, tq, QK_DIM), lambda b, h, i: (h, b * nq + i, 0)),
        pl.BlockSpec((hb, seq_len, QK_DIM), lambda b, h, i: (h, b, 0)),
        pl.BlockSpec((hb, seq_len, VO_DIM), lambda b, h, i: (h, b, 0)),
    ]
    args = [q, k, v]
    if has_ctx:
        past = ctx[0].shape[1] // batch
        in_specs += [pl.BlockSpec((hb, past, QK_DIM), lambda b, h, i: (h, b, 0)),
                     pl.BlockSpec((hb, past, VO_DIM), lambda b, h, i: (h, b, 0))]
        args += list(ctx)
    return pl.pallas_call(
        functools.partial(_attn_kernel, heads=hb, has_ctx=has_ctx, kc=kc),
        out_shape=jax.ShapeDtypeStruct((t, N_HEADS * V_DIM), BF16),
        grid=(batch, N_HEADS // hb, nq),
        in_specs=in_specs,
        out_specs=pl.BlockSpec((tq, hb * V_DIM), lambda b, h, i: (b * nq + i, h)),
        compiler_params=_params("parallel", "parallel", "arbitrary"),
        name="attention",
    )(*args)


CONV_ROWS = 64


def _conv_copy_rows(seg):
    return seg + (HALO + CONV_K // 2) // SUBLANE * SUBLANE


def _out_proj_kernel(*refs, tm, seg, halo, tiles_per_seq):
    x_ref, mod_ref, a_ref = refs[:3]
    refs = refs[3:]
    if halo:
        ap_ref, an_ref = refs[:2]
        refs = refs[2:]
    b_ref, o_ref, cw_ref, cb_ref, lg_ref, lb_ref, wout_ref, y_ref, abuf, sh, aout, cat = refs

    cw = cw_ref[...]
    ext = _conv_copy_rows(seg)
    for sgm in range(tm // seg):
        r0 = sgm * seg
        if halo:
            ti = pl.program_id(0) % tiles_per_seq
            abuf[0:HALO, :] = ap_ref[...].astype(F32) * jnp.where(ti > 0, 1.0, 0.0)
            abuf[HALO + seg:, :] = an_ref[...].astype(F32) * jnp.where(ti < tiles_per_seq - 1, 1.0, 0.0)
        else:
            abuf[0:HALO, :] = jnp.zeros((HALO, D_A), F32)
            abuf[HALO + seg:, :] = jnp.zeros((HALO, D_A), F32)
        abuf[HALO:HALO + seg, :] = a_ref[r0:r0 + seg, :].astype(F32)
        for r in range(1, SUBLANE):
            sh[r - 1, :, :] = abuf[r:r + ext, :]
        for t0 in range(0, seg, CONV_ROWS):
            acc = jnp.zeros((CONV_ROWS, D_A), F32) + cb_ref[...]
            for kk in range(CONV_K):
                q8, r = divmod(HALO - CONV_K // 2 + kk, SUBLANE)
                rows = slice(t0 + q8 * SUBLANE, t0 + q8 * SUBLANE + CONV_ROWS)
                src = abuf[rows, :] if r == 0 else sh[r - 1, rows, :]
                acc = acc + src * cw[kk:kk + 1, :]
            an = _group_ln(acc, lg_ref[...], lb_ref[...], G_A)
            aout[r0 + t0:r0 + t0 + CONV_ROWS, :] = (an * _sigmoid(an)).astype(BF16)

    gate = mod_ref[0][2:3]
    cat[:, 0:D_B] = b_ref[...]
    cat[:, D_B:] = o_ref[...]
    y_ref[...] = x_ref[...] + gate * _dot(cat[...], wout_ref[D_A:, :])
    y_ref[...] += gate * _dot(aout[...], wout_ref[0:D_A, :])


def _out_proj(x, mod, a, b, o, w, l, *, seq_len, tm):
    t = x.shape[0]
    halo = tm < seq_len
    assert (seq_len % tm == 0) if halo else (tm % seq_len == 0)
    seg = tm if halo else seq_len
    tiles_per_seq = max(seq_len // tm, 1)
    hb = tm // HALO
    n_hb = t // HALO
    n_mod = mod.shape[0]

    def mod_idx(i):
        return (jnp.minimum(i * tm // seq_len, n_mod - 1), 0, 0)

    d_mix = D_A + D_B + N_HEADS * V_DIM
    in_specs = [
        pl.BlockSpec((tm, D_MODEL), lambda i: (i, 0)),
        pl.BlockSpec((1, 6, D_MODEL), mod_idx),
        pl.BlockSpec((tm, D_A), lambda i: (i, 0)),
    ]
    args = [x, mod, a]
    if halo:
        in_specs += [pl.BlockSpec((HALO, D_A), lambda i: (jnp.maximum(i * hb - 1, 0), 0)),
                     pl.BlockSpec((HALO, D_A), lambda i: (jnp.minimum((i + 1) * hb, n_hb - 1), 0))]
        args += [a, a]
    in_specs += [
        pl.BlockSpec((tm, D_B), lambda i: (i, 0)),
        pl.BlockSpec((tm, N_HEADS * V_DIM), lambda i: (i, 0)),
        _layer_spec((CONV_K, D_A), l), _layer_spec((1, D_A), l), _layer_spec((1, D_A), l),
        _layer_spec((1, D_A), l),
        _layer_spec((d_mix, D_MODEL), l),
    ]
    args += [b, o, w["conv_w"], w["conv_b"], w["conv_ln_g"], w["conv_ln_b"], w["w_out"]]
    return pl.pallas_call(
        functools.partial(_out_proj_kernel, tm=tm, seg=seg, halo=halo, tiles_per_seq=tiles_per_seq),
        out_shape=jax.ShapeDtypeStruct((t, D_MODEL), F32),
        grid=(t // tm,),
        in_specs=in_specs,
        out_specs=pl.BlockSpec((tm, D_MODEL), lambda i: (i, 0)),
        scratch_shapes=[pltpu.VMEM((seg + 2 * HALO, D_A), F32),
                        pltpu.VMEM((SUBLANE - 1, _conv_copy_rows(seg), D_A), F32),
                        pltpu.VMEM((tm, D_A), BF16),
                        pltpu.VMEM((tm, d_mix - D_A), BF16)],
        compiler_params=_params("parallel"),
        name="out_proj",
    )(*args)


def _ffn_kernel(*refs, tm, seq_len, halo, final_norm):
    x_ref = refs[0]
    refs = refs[1:]
    if halo:
        xp_ref, xn_ref = refs[:2]
        refs = refs[2:]
    (mod_ref, g2_ref, wg_ref, wv_ref, cwg_ref, cwv_ref, cbg_ref, cbv_ref, wdn_ref, fg_ref,
     y_ref, hs, zg, zv) = refs
    i = pl.program_id(0)
    j = pl.program_id(1)
    nj = pl.num_programs(1)
    mod = mod_ref[0]
    pad = HALO if halo else SUBLANE
    tf = zg.shape[1]

    def prologue():
        g2 = g2_ref[...]
        scale = 1.0 + mod[4:5]
        shift = mod[3:4]
        x = x_ref[...]
        if halo:
            tiles_per_seq = seq_len // tm
            ti = i % tiles_per_seq
            prev_ok = jnp.where(ti > 0, 1.0, 0.0)
            next_ok = jnp.where(ti < tiles_per_seq - 1, 1.0, 0.0)
            hs[0:HALO, :] = ((_rmsnorm(xp_ref[...], g2) * scale + shift) * prev_ok).astype(BF16)
            hs[HALO:HALO + tm, :] = (_rmsnorm(x, g2) * scale + shift).astype(BF16)
            hs[HALO + tm:, :] = ((_rmsnorm(xn_ref[...], g2) * scale + shift) * next_ok).astype(BF16)
        else:
            hs[...] = (_rmsnorm(x, g2) * scale + shift).astype(BF16)
            zeros = jnp.zeros((pad, tf), F32)
            for z in (zg, zv):
                z[0:pad, :] = zeros
                z[pad + tm:, :] = zeros
        y_ref[...] = jnp.zeros_like(y_ref)

    def ff_tile():
        if halo:
            zg[...] = _dot(hs[...], wg_ref[...])
            zv[...] = _dot(hs[...], wv_ref[...])
        else:
            zg[pad:pad + tm, :] = _dot(hs[...], wg_ref[...])
            zv[pad:pad + tm, :] = _dot(hs[...], wv_ref[...])
            pos = lax.rem(lax.broadcasted_iota(jnp.int32, (tm, 1), 0), seq_len)
            not_first = pos != 0
            not_last = pos != seq_len - 1

        def conv(z, cw_ref, cb_ref):
            cw = cw_ref[...]
            zp = z[pad - 1:pad - 1 + tm, :]
            zn = z[pad + 1:pad + 1 + tm, :]
            if not halo:
                zp = jnp.where(not_first, zp, 0.0)
                zn = jnp.where(not_last, zn, 0.0)
            return cb_ref[...] + zp * cw[0:1, :] + z[pad:pad + tm, :] * cw[1:2, :] + zn * cw[2:3, :]

        g = conv(zg, cwg_ref, cbg_ref)
        act = (g * _sigmoid(g) * conv(zv, cwv_ref, cbv_ref)).astype(BF16)
        y_ref[...] += _dot(act, wdn_ref[...])

    @pl.when(j == 0)
    def _():
        prologue()
        ff_tile()

    @pl.when(j > 0)
    def _():
        ff_tile()

    @pl.when(j == nj - 1)
    def _():
        y = x_ref[...] + mod[5:6] * y_ref[...]
        if final_norm:
            y = _rmsnorm(y, fg_ref[...])
        y_ref[...] = y


def _ffn(x, mod, w, l, final_g, *, seq_len, tm, tf, final_norm):
    t = x.shape[0]
    halo = tm < seq_len
    assert (seq_len % tm == 0) if halo else (tm % seq_len == 0)
    nf = D_FF // tf
    hb = tm // HALO
    n_hb = t // HALO
    n_mod = mod.shape[0]
    pad = HALO if halo else SUBLANE

    def mod_idx(i, j):
        return (jnp.minimum(i * tm // seq_len, n_mod - 1), 0, 0)

    def lspec(shape, idx):
        return pl.BlockSpec((None,) + shape, lambda i, j: (l,) + idx(i, j))

    in_specs = [pl.BlockSpec((tm, D_MODEL), lambda i, j: (i, 0))]
    args = [x]
    if halo:
        in_specs += [
            pl.BlockSpec((HALO, D_MODEL), lambda i, j: (jnp.maximum(i * hb - 1, 0), 0)),
            pl.BlockSpec((HALO, D_MODEL), lambda i, j: (jnp.minimum((i + 1) * hb, n_hb - 1), 0)),
        ]
        args += [x, x]
    in_specs += [
        pl.BlockSpec((1, 6, D_MODEL), mod_idx),
        _layer_spec((1, D_MODEL), l),
        lspec((D_MODEL, tf), lambda i, j: (0, j)),
        lspec((D_MODEL, tf), lambda i, j: (0, nf + j)),
        lspec((FFN_K, tf), lambda i, j: (0, j)),
        lspec((FFN_K, tf), lambda i, j: (0, nf + j)),
        lspec((1, tf), lambda i, j: (0, j)),
        lspec((1, tf), lambda i, j: (0, nf + j)),
        lspec((tf, D_MODEL), lambda i, j: (j, 0)),
        pl.BlockSpec((1, D_MODEL), lambda i, j: (0, 0)),
    ]
    args += [mod, w["norm2_g"], w["w_up"], w["w_up"], w["ffn_conv_w"], w["ffn_conv_w"],
             w["ffn_conv_b"], w["ffn_conv_b"], w["w_down"], final_g]
    h_rows = tm + 2 * HALO if halo else tm
    return pl.pallas_call(
        functools.partial(_ffn_kernel, tm=tm, seq_len=seq_len, halo=halo, final_norm=final_norm),
        out_shape=jax.ShapeDtypeStruct((t, D_MODEL), F32),
        grid=(t // tm, nf),
        in_specs=in_specs,
        out_specs=pl.BlockSpec((tm, D_MODEL), lambda i, j: (i, 0)),
        scratch_shapes=[pltpu.VMEM((h_rows, D_MODEL), BF16),
                        pltpu.VMEM((tm + 2 * pad, tf), F32),
                        pltpu.VMEM((tm + 2 * pad, tf), F32)],
        compiler_params=_params("parallel", "arbitrary"),
        name="conv_ffn",
    )(*args)


def _rope_tables(length):
    pos = jnp.arange(length)
    r = (pos // GRID_W).astype(F32)
    col = (pos % GRID_W).astype(F32)
    half = ROPE_DIM // 4
    freq = 1.0 / (ROPE_THETA ** (jnp.arange(half, dtype=F32) / half))
    ar = r[:, None] * freq
    ac = col[:, None] * freq
    zeros = jnp.zeros((length, LANE - ROPE_DIM), F32)
    cos = jnp.concatenate([jnp.cos(ar), jnp.cos(ar), jnp.cos(ac), jnp.cos(ac), zeros], axis=1)
    sin = jnp.concatenate([-jnp.sin(ar), jnp.sin(ar), -jnp.sin(ac), jnp.sin(ac), zeros], axis=1)
    return cos, sin


def _swap_rope_cols(w):
    q = ROPE_DIM // 4
    return jnp.concatenate([w[..., q:2 * q], w[..., 0:q], w[..., 3 * q:4 * q], w[..., 2 * q:3 * q]], axis=-1)


def _prep_weights(p):
    w_in = p["w_in"]
    zkr_w = w_in[:, :, C_ZKR:C_ZKR + ROPE_DIM]
    zpad = jnp.zeros((DEPTH, D_MODEL, LANE - ROPE_DIM), F32)
    w_in_ext = jnp.concatenate([w_in[:, :, :C_ZKR], zkr_w, zpad, _swap_rope_cols(zkr_w), zpad], axis=-1)

    wuq = p["w_uq"].reshape(DEPTH, Q_RANK, N_HEADS, NOPE_DIM + ROPE_DIM)
    nope = wuq[..., :NOPE_DIM].reshape(DEPTH, Q_RANK, HW)
    rope = wuq[..., NOPE_DIM:]
    rope_p = jnp.concatenate([rope, _swap_rope_cols(rope)], axis=-1).reshape(DEPTH, Q_RANK, HW)

    wukv = p["w_ukv"].reshape(DEPTH, KV_RANK, N_HEADS, NOPE_DIM + V_DIM)
    wukv = jnp.concatenate([wukv[..., :NOPE_DIM].reshape(DEPTH, KV_RANK, HW),
                            wukv[..., NOPE_DIM:].reshape(DEPTH, KV_RANK, HW)], axis=-1)

    def vec(name):
        return p[name][:, None, :]

    return {
        "norm1_g": vec("norm1_g"),
        "w_in": w_in_ext.astype(BF16),
        "sg_ln_g": vec("sg_ln_g"), "sg_ln_b": vec("sg_ln_b"),
        "sg_w": p["sg_w"].astype(BF16),
        "sg_bias": jnp.broadcast_to(p["sg_b"][..., None], (DEPTH, G_B, CHUNK, LANE)),
        "q_norm_g": vec("q_norm_g"),
        "w_uq": jnp.concatenate([nope, rope_p], axis=-1).astype(BF16),
        "kv_norm_g": vec("kv_norm_g"),
        "w_ukv": wukv.astype(BF16),
        "conv_w": p["conv_w"], "conv_b": vec("conv_b"),
        "conv_ln_g": vec("conv_ln_g"), "conv_ln_b": vec("conv_ln_b"),
        "w_out": p["w_out"].astype(BF16),
        "norm2_g": vec("norm2_g"),
        "w_up": p["ffn_w_up"].astype(BF16),
        "ffn_conv_w": p["ffn_conv_w"], "ffn_conv_b": vec("ffn_conv_b"),
        "w_down": p["ffn_w_down"].astype(BF16),
    }


TM_IN = 512
TM_OUT = 512
TM_FFN = 1024
TF_FFN = 512
TQ_LAT = 1024
KC_LAT = 1024


def kernel(x_prompt, x_sample, cache_ckv, cache_krope, c, c_ctx, norm1_g, w_ada, b_ada, w_in, conv_w, conv_b, conv_ln_g, conv_ln_b, sg_ln_g, sg_ln_b, sg_w, sg_b, q_norm_g, w_uq, kv_norm_g, w_ukv, w_out, norm2_g, ffn_w_up, ffn_conv_w, ffn_conv_b, ffn_w_down, final_g):
    w = _prep_weights(dict(
        norm1_g=norm1_g, w_in=w_in, conv_w=conv_w, conv_b=conv_b, conv_ln_g=conv_ln_g,
        conv_ln_b=conv_ln_b, sg_ln_g=sg_ln_g, sg_ln_b=sg_ln_b, sg_w=sg_w, sg_b=sg_b,
        q_norm_g=q_norm_g, w_uq=w_uq, kv_norm_g=kv_norm_g, w_ukv=w_ukv, w_out=w_out,
        norm2_g=norm2_g, ffn_w_up=ffn_w_up, ffn_conv_w=ffn_conv_w, ffn_conv_b=ffn_conv_b,
        ffn_w_down=ffn_w_down))
    bp, lp_, _ = x_prompt.shape
    bs, ls, _ = x_sample.shape

    cvec = jnp.concatenate([c_ctx[None], c, jnp.zeros((8 - 1 - bs, D_MODEL), F32)], axis=0)
    mod_all = _ada_mod(cvec, w_ada, b_ada).reshape(DEPTH, 8, 6, D_MODEL)
    rope = _rope_tables(ls)
    fg = final_g[None]

    xp = x_prompt.reshape(bp * lp_, D_MODEL)
    xs = x_sample.reshape(bs * ls, D_MODEL)
    ckv_out, kr_out = [], []
    for l in range(DEPTH):
        last = l == DEPTH - 1
        mod_ctx = mod_all[l, 0:1]
        mod_lat = mod_all[l, 1:1 + bs]

        a, b, q, k, v, ckv_n, kr = _in_proj(xp, mod_ctx, w, l, seq_len=lp_, rope=None, emit_cache=True, tm=TM_IN)
        ckv_out.append(ckv_n.reshape(bp, lp_, KV_RANK))
        kr_out.append(kr.reshape(bp, lp_, ROPE_DIM))
        o = _attention(q, k, v, None, batch=bp, seq_len=lp_, tq=lp_, kc=lp_, heads_per_step=N_HEADS)
        xp = _out_proj(xp, mod_ctx, a, b, o, w, l, seq_len=lp_, tm=TM_OUT)
        xp = _ffn(xp, mod_ctx, w, l, fg, seq_len=lp_, tm=TM_FFN, tf=TF_FFN, final_norm=last)

        ctx = _ctx_kv(cache_ckv, cache_krope, w, l)
        a, b, q, k, v = _in_proj(xs, mod_lat, w, l, seq_len=ls, rope=rope, emit_cache=False, tm=TM_IN)
        o = _attention(q, k, v, ctx, batch=bs, seq_len=ls, tq=TQ_LAT, kc=KC_LAT, heads_per_step=2)
        xs = _out_proj(xs, mod_lat, a, b, o, w, l, seq_len=ls, tm=TM_OUT)
        xs = _ffn(xs, mod_lat, w, l, fg, seq_len=ls, tm=TM_FFN, tf=TF_FFN, final_norm=last)

    return (xp.reshape(bp, lp_, D_MODEL), xs.reshape(bs, ls, D_MODEL),
            jnp.stack(ckv_out, axis=1), jnp.stack(kr_out, axis=1))
```

```python
import functools

import jax
import jax.numpy as jnp
import numpy as np
from jax import lax
from jax.experimental import pallas as pl
from jax.experimental.pallas import tpu as pltpu

D_MODEL = 2048
DEPTH = 2
GRID_W = 64
D_A = 512
G_A = 4
CONV_K = 31
D_B = 512
G_B = 4
CHUNK = 128
NOPE_DIM = 128
ROPE_DIM = 64
V_DIM = 128
N_HEADS = 8
Q_RANK = 512
KV_RANK = 256
D_FF = 5632
FFN_K = 3
ROPE_THETA = 10000.0
EPS = 1e-6
ATTN_SCALE = (NOPE_DIM + ROPE_DIM) ** -0.5

LANE = 128
SUBLANE = 8
QK_DIM = 2 * LANE
VO_DIM = 2 * LANE
Q_SCALE = ATTN_SCALE * 1.4426950408889634
HALO = 16
VMEM_LIMIT = 62 * 1024 * 1024

C_ZA = 0
C_ZB = 2 * D_A
C_ZQ = C_ZB + 2 * D_B
C_ZKV = C_ZQ + Q_RANK
C_ZKR = C_ZKV + KV_RANK
N_IN_EXT = C_ZKR + 2 * LANE
HW = N_HEADS * LANE

F32 = jnp.float32
BF16 = jnp.bfloat16


def _dot(a, b):
    return jnp.dot(a, b, preferred_element_type=F32)


def _rmsnorm(x, g):
    return x * lax.rsqrt(jnp.mean(x * x, axis=-1, keepdims=True) + EPS) * g


def _group_ln(x, g, b, groups):
    outs = []
    for i in range(groups):
        xg = x[:, i * LANE:(i + 1) * LANE]
        mu = jnp.mean(xg, axis=-1, keepdims=True)
        xc = xg - mu
        var = jnp.mean(xc * xc, axis=-1, keepdims=True)
        outs.append(xc * lax.rsqrt(var + EPS))
    return jnp.concatenate(outs, axis=-1) * g + b


def _sigmoid(x):
    return 1.0 / (1.0 + jnp.exp(-x))


def _layer_spec(shape, l):
    nd = len(shape)
    return pl.BlockSpec((None,) + tuple(shape), lambda *_: (l,) + (0,) * nd, pipeline_mode=pl.Buffered(1))


def _params(*sem):
    return pltpu.CompilerParams(dimension_semantics=sem, vmem_limit_bytes=VMEM_LIMIT)


def _ada_kernel(c_ref, w_ref, b_ref, o_ref):
    c = c_ref[...]
    s = (c * _sigmoid(c)).astype(BF16)
    o_ref[0] = _dot(s, w_ref[0].astype(BF16)) + b_ref[0]


def _ada_mod(cvec, w_ada, b_ada):
    tn = 1024
    n = 6 * D_MODEL
    return pl.pallas_call(
        _ada_kernel,
        out_shape=jax.ShapeDtypeStruct((DEPTH, 8, n), F32),
        grid=(DEPTH, n // tn),
        in_specs=[
            pl.BlockSpec((8, D_MODEL), lambda l, j: (0, 0)),
            pl.BlockSpec((1, D_MODEL, tn), lambda l, j: (l, 0, j)),
            pl.BlockSpec((1, 1, tn), lambda l, j: (l, 0, j)),
        ],
        out_specs=pl.BlockSpec((1, 8, tn), lambda l, j: (l, 0, j)),
        compiler_params=_params("arbitrary", "arbitrary"),
        name="ada_mod",
    )(cvec, w_ada, b_ada.reshape(DEPTH, 1, n))


def _in_proj_kernel(*refs, tm, use_rope, emit_cache):
    (x_ref, mod_ref, g1_ref, win_ref, sgg_ref, sgb_ref, sgw_ref, sgbias_ref,
     qg_ref, wuq_ref, kvg_ref, wukv_ref) = refs[:12]
    refs = refs[12:]
    if use_rope:
        cos_ref, sin_ref = refs[:2]
        refs = refs[2:]
    a_ref, b_ref, q_ref, k_ref, v_ref = refs[:5]
    refs = refs[5:]
    if emit_cache:
        ckv_ref, kr_ref = refs

    mod = mod_ref[0]
    h = _rmsnorm(x_ref[...], g1_ref[...]) * (1.0 + mod[1:2]) + mod[0:1]
    h = h.astype(BF16)

    za = _dot(h, win_ref[:, C_ZA:C_ZA + 2 * D_A])
    a_ref[...] = (za[:, :D_A] * _sigmoid(za[:, D_A:])).astype(BF16)

    zb = _dot(h, win_ref[:, C_ZB:C_ZB + 2 * D_B])
    zb = 0.5 * zb * (1.0 + lax.erf(zb * (2.0 ** -0.5)))
    u = zb[:, :D_B]
    vn = _group_ln(zb[:, D_B:], sgg_ref[...], sgb_ref[...], G_B).astype(BF16)
    for c in range(tm // CHUNK):
        rows = slice(c * CHUNK, (c + 1) * CHUNK)
        for g in range(G_B):
            cols = slice(g * LANE, (g + 1) * LANE)
            sv = _dot(sgw_ref[g], vn[rows, cols]) + sgbias_ref[g]
            b_ref[rows, cols] = (u[rows, cols] * sv).astype(BF16)

    zq = _dot(h, win_ref[:, C_ZQ:C_ZQ + Q_RANK])
    qn = _rmsnorm(zq, qg_ref[...]).astype(BF16)
    q = _dot(qn, wuq_ref[...])
    if use_rope:
        cos = cos_ref[...]
        sin = sin_ref[...]
    else:
        rope_lanes = lax.broadcasted_iota(jnp.int32, (1, LANE), 1) < ROPE_DIM
    for hd in range(N_HEADS):
        cols = slice(hd * LANE, (hd + 1) * LANE)
        q_ref[hd, :, 0:LANE] = (q[:, cols] * Q_SCALE).astype(BF16)
        qr = q[:, HW:2 * HW][:, cols]
        if use_rope:
            qr = qr * cos + pltpu.roll(qr, ROPE_DIM, axis=1) * sin
        else:
            qr = jnp.where(rope_lanes, qr, 0.0)
        q_ref[hd, :, LANE:QK_DIM] = (qr * Q_SCALE).astype(BF16)

    nkv = KV_RANK + (2 * LANE if use_rope else LANE)
    zkv = _dot(h, win_ref[:, C_ZKV:C_ZKV + nkv])
    ckv_n = _rmsnorm(zkv[:, :KV_RANK], kvg_ref[...])
    kr = zkv[:, KV_RANK:KV_RANK + LANE]
    if emit_cache:
        ckv_ref[...] = ckv_n
        kr_ref[...] = kr[:, :ROPE_DIM]
    if use_rope:
        kr = kr * cos + zkv[:, KV_RANK + LANE:KV_RANK + 2 * LANE] * sin
    kr = kr.astype(BF16)
    kv = _dot(ckv_n.astype(BF16), wukv_ref[...])
    ones = jnp.ones((tm, VO_DIM - V_DIM), BF16)
    for hd in range(N_HEADS):
        cols = slice(hd * LANE, (hd + 1) * LANE)
        k_ref[hd, :, 0:LANE] = kv[:, cols].astype(BF16)
        k_ref[hd, :, LANE:QK_DIM] = kr
        v_ref[hd, :, 0:V_DIM] = kv[:, HW:2 * HW][:, cols].astype(BF16)
        v_ref[hd, :, V_DIM:VO_DIM] = ones


def _in_proj(x, mod, w, l, *, seq_len, rope, emit_cache, tm):
    t = x.shape[0]
    use_rope = rope is not None
    tiles_per_seq = max(seq_len // tm, 1)
    n_mod = mod.shape[0]

    def mod_idx(i):
        return (jnp.minimum(i * tm // seq_len, n_mod - 1), 0, 0)

    in_specs = [
        pl.BlockSpec((tm, D_MODEL), lambda i: (i, 0)),
        pl.BlockSpec((1, 6, D_MODEL), mod_idx),
        _layer_spec((1, D_MODEL), l),
        _layer_spec((D_MODEL, N_IN_EXT), l),
        _layer_spec((1, D_B), l), _layer_spec((1, D_B), l),
        _layer_spec((G_B, CHUNK, CHUNK), l), _layer_spec((G_B, CHUNK, LANE), l),
        _layer_spec((1, Q_RANK), l),
        _layer_spec((Q_RANK, 2 * HW), l),
        _layer_spec((1, KV_RANK), l),
        _layer_spec((KV_RANK, 2 * HW), l),
    ]
    args = [x, mod, w["norm1_g"], w["w_in"], w["sg_ln_g"], w["sg_ln_b"], w["sg_w"], w["sg_bias"],
            w["q_norm_g"], w["w_uq"], w["kv_norm_g"], w["w_ukv"]]
    if use_rope:
        in_specs += [pl.BlockSpec((tm, LANE), lambda i: (i % tiles_per_seq, 0))] * 2
        args += list(rope)
    out_shape = [
        jax.ShapeDtypeStruct((t, D_A), BF16),
        jax.ShapeDtypeStruct((t, D_B), BF16),
        jax.ShapeDtypeStruct((N_HEADS, t, QK_DIM), BF16),
        jax.ShapeDtypeStruct((N_HEADS, t, QK_DIM), BF16),
        jax.ShapeDtypeStruct((N_HEADS, t, VO_DIM), BF16),
    ]
    out_specs = [
        pl.BlockSpec((tm, D_A), lambda i: (i, 0)),
        pl.BlockSpec((tm, D_B), lambda i: (i, 0)),
        pl.BlockSpec((N_HEADS, tm, QK_DIM), lambda i: (0, i, 0)),
        pl.BlockSpec((N_HEADS, tm, QK_DIM), lambda i: (0, i, 0)),
        pl.BlockSpec((N_HEADS, tm, VO_DIM), lambda i: (0, i, 0)),
    ]
    if emit_cache:
        out_shape += [jax.ShapeDtypeStruct((t, KV_RANK), F32), jax.ShapeDtypeStruct((t, ROPE_DIM), F32)]
        out_specs += [pl.BlockSpec((tm, KV_RANK), lambda i: (i, 0)),
                      pl.BlockSpec((tm, ROPE_DIM), lambda i: (i, 0))]
    return pl.pallas_call(
        functools.partial(_in_proj_kernel, tm=tm, use_rope=use_rope, emit_cache=emit_cache),
        out_shape=out_shape,
        grid=(t // tm,),
        in_specs=in_specs,
        out_specs=out_specs,
        compiler_params=_params("parallel"),
        name="in_proj",
    )(*args)


def _ctx_kv_kernel(ckv_ref, kr_ref, wukv_ref, k_ref, v_ref):
    kv = _dot(ckv_ref[...].astype(BF16), wukv_ref[...])
    kr = kr_ref[...].astype(BF16)
    zeros = jnp.zeros((kr.shape[0], LANE - ROPE_DIM), BF16)
    ones = jnp.ones((kr.shape[0], VO_DIM - V_DIM), BF16)
    for hd in range(N_HEADS):
        cols = slice(hd * LANE, (hd + 1) * LANE)
        k_ref[hd, :, 0:LANE] = kv[:, cols].astype(BF16)
        k_ref[hd, :, LANE:LANE + ROPE_DIM] = kr
        k_ref[hd, :, LANE + ROPE_DIM:QK_DIM] = zeros
        v_ref[hd, :, 0:V_DIM] = kv[:, HW:2 * HW][:, cols].astype(BF16)
        v_ref[hd, :, V_DIM:VO_DIM] = ones


def _ctx_kv(cache_ckv, cache_krope, w, l):
    bsz, _, past, _ = cache_ckv.shape
    return pl.pallas_call(
        _ctx_kv_kernel,
        out_shape=[jax.ShapeDtypeStruct((N_HEADS, bsz * past, QK_DIM), BF16),
                   jax.ShapeDtypeStruct((N_HEADS, bsz * past, VO_DIM), BF16)],
        grid=(bsz,),
        in_specs=[pl.BlockSpec((None, None, past, KV_RANK), lambda b: (b, l, 0, 0)),
                  pl.BlockSpec((None, None, past, ROPE_DIM), lambda b: (b, l, 0, 0)),
                  _layer_spec((KV_RANK, 2 * HW), l)],
        out_specs=[pl.BlockSpec((N_HEADS, past, QK_DIM), lambda b: (0, b, 0)),
                   pl.BlockSpec((N_HEADS, past, VO_DIM), lambda b: (0, b, 0))],
        compiler_params=_params("parallel"),
        name="ctx_kv",
    )(cache_ckv, cache_krope, w["w_ukv"])


def _attn_kernel(*refs, heads, has_ctx, kc):
    if has_ctx:
        q_ref, kl_ref, vl_ref, kx_ref, vx_ref, o_ref = refs
    else:
        q_ref, kl_ref, vl_ref, o_ref = refs
    for hd in range(heads):
        q = q_ref[hd]
        chunks = []
        if has_ctx:
            chunks.append((kx_ref, vx_ref, 0, kx_ref.shape[1]))
        n_keys = kl_ref.shape[1]
        for s0 in range(0, n_keys, kc):
            chunks.append((kl_ref, vl_ref, s0, min(kc, n_keys - s0)))
        m = pv = None
        for (k_r, v_r, s0, n) in chunks:
            k = k_r[hd, s0:s0 + n, :]
            v = v_r[hd, s0:s0 + n, :]
            s = lax.dot_general(q, k, (((1,), (1,)), ((), ())), preferred_element_type=F32)
            m_c = jnp.max(s, axis=-1, keepdims=True)
            if m is None:
                m = m_c
                pv = _dot(jnp.exp2(s - m).astype(BF16), v)
            else:
                m_new = jnp.maximum(m, m_c)
                pv = jnp.exp2(m - m_new) * pv + _dot(jnp.exp2(s - m_new).astype(BF16), v)
                m = m_new
        o_ref[:, hd * V_DIM:(hd + 1) * V_DIM] = (pv[:, :V_DIM] / pv[:, V_DIM:]).astype(o_ref.dtype)


def _attention(q, k, v, ctx, *, batch, seq_len, tq, kc, heads_per_step):
    t = q.shape[1]
    nq = seq_len // tq
    hb = heads_per_step
    has_ctx = ctx is not None
    in_specs = [
        pl.BlockSpec((hb, tq, QK_DIM), lambda b, h, i: (h, b * nq + i, 0)),
        pl.BlockSpec((hb, seq_len, QK_DIM), lambda b, h, i: (h, b, 0)),
        pl.BlockSpec((hb, seq_len, VO_DIM), lambda b, h, i: (h, b, 0)),
    ]
    args = [q, k, v]
    if has_ctx:
        past = ctx[0].shape[1] // batch
        in_specs += [pl.BlockSpec((hb, past, QK_DIM), lambda b, h, i: (h, b, 0)),
                     pl.BlockSpec((hb, past, VO_DIM), lambda b, h, i: (h, b, 0))]
        args += list(ctx)
    return pl.pallas_call(
        functools.partial(_attn_kernel, heads=hb, has_ctx=has_ctx, kc=kc),
        out_shape=jax.ShapeDtypeStruct((t, N_HEADS * V_DIM), BF16),
        grid=(batch, N_HEADS // hb, nq),
        in_specs=in_specs,
        out_specs=pl.BlockSpec((tq, hb * V_DIM), lambda b, h, i: (b * nq + i, h)),
        compiler_params=_params("parallel", "parallel", "arbitrary"),
        name="attention",
    )(*args)


CONV_ROWS = 64


def _conv_copy_rows(seg):
    return seg + (HALO + CONV_K // 2) // SUBLANE * SUBLANE


def _out_proj_kernel(*refs, tm, seg, halo, tiles_per_seq):
    x_ref, mod_ref, a_ref = refs[:3]
    refs = refs[3:]
    if halo:
        ap_ref, an_ref = refs[:2]
        refs = refs[2:]
    b_ref, o_ref, cw_ref, cb_ref, lg_ref, lb_ref, wout_ref, y_ref, abuf, sh, aout, cat = refs

    cw = cw_ref[...]
    ext = _conv_copy_rows(seg)
    for sgm in range(tm // seg):
        r0 = sgm * seg
        if halo:
            ti = pl.program_id(0) % tiles_per_seq
            abuf[0:HALO, :] = ap_ref[...].astype(F32) * jnp.where(ti > 0, 1.0, 0.0)
            abuf[HALO + seg:, :] = an_ref[...].astype(F32) * jnp.where(ti < tiles_per_seq - 1, 1.0, 0.0)
        else:
            abuf[0:HALO, :] = jnp.zeros((HALO, D_A), F32)
            abuf[HALO + seg:, :] = jnp.zeros((HALO, D_A), F32)
        abuf[HALO:HALO + seg, :] = a_ref[r0:r0 + seg, :].astype(F32)
        for r in range(1, SUBLANE):
            sh[r - 1, :, :] = abuf[r:r + ext, :]
        for t0 in range(0, seg, CONV_ROWS):
            acc = jnp.zeros((CONV_ROWS, D_A), F32) + cb_ref[...]
            for kk in range(CONV_K):
                q8, r = divmod(HALO - CONV_K // 2 + kk, SUBLANE)
                rows = slice(t0 + q8 * SUBLANE, t0 + q8 * SUBLANE + CONV_ROWS)
                src = abuf[rows, :] if r == 0 else sh[r - 1, rows, :]
                acc = acc + src * cw[kk:kk + 1, :]
            an = _group_ln(acc, lg_ref[...], lb_ref[...], G_A)
            aout[r0 + t0:r0 + t0 + CONV_ROWS, :] = (an * _sigmoid(an)).astype(BF16)

    gate = mod_ref[0][2:3]
    cat[:, 0:D_B] = b_ref[...]
    cat[:, D_B:] = o_ref[...]
    y_ref[...] = x_ref[...] + gate * _dot(cat[...], wout_ref[D_A:, :])
    y_ref[...] += gate * _dot(aout[...], wout_ref[0:D_A, :])


def _out_proj(x, mod, a, b, o, w, l, *, seq_len, tm):
    t = x.shape[0]
    halo = tm < seq_len
    assert (seq_len % tm == 0) if halo else (tm % seq_len == 0)
    seg = tm if halo else seq_len
    tiles_per_seq = max(seq_len // tm, 1)
    hb = tm // HALO
    n_hb = t // HALO
    n_mod = mod.shape[0]

    def mod_idx(i):
        return (jnp.minimum(i * tm // seq_len, n_mod - 1), 0, 0)

    d_mix = D_A + D_B + N_HEADS * V_DIM
    in_specs = [
        pl.BlockSpec((tm, D_MODEL), lambda i: (i, 0)),
        pl.BlockSpec((1, 6, D_MODEL), mod_idx),
        pl.BlockSpec((tm, D_A), lambda i: (i, 0)),
    ]
    args = [x, mod, a]
    if halo:
        in_specs += [pl.BlockSpec((HALO, D_A), lambda i: (jnp.maximum(i * hb - 1, 0), 0)),
                     pl.BlockSpec((HALO, D_A), lambda i: (jnp.minimum((i + 1) * hb, n_hb - 1), 0))]
        args += [a, a]
    in_specs += [
        pl.BlockSpec((tm, D_B), lambda i: (i, 0)),
        pl.BlockSpec((tm, N_HEADS * V_DIM), lambda i: (i, 0)),
        _layer_spec((CONV_K, D_A), l), _layer_spec((1, D_A), l), _layer_spec((1, D_A), l),
        _layer_spec((1, D_A), l),
        _layer_spec((d_mix, D_MODEL), l),
    ]
    args += [b, o, w["conv_w"], w["conv_b"], w["conv_ln_g"], w["conv_ln_b"], w["w_out"]]
    return pl.pallas_call(
        functools.partial(_out_proj_kernel, tm=tm, seg=seg, halo=halo, tiles_per_seq=tiles_per_seq),
        out_shape=jax.ShapeDtypeStruct((t, D_MODEL), F32),
        grid=(t // tm,),
        in_specs=in_specs,
        out_specs=pl.BlockSpec((tm, D_MODEL), lambda i: (i, 0)),
        scratch_shapes=[pltpu.VMEM((seg + 2 * HALO, D_A), F32),
                        pltpu.VMEM((SUBLANE - 1, _conv_copy_rows(seg), D_A), F32),
                        pltpu.VMEM((tm, D_A), BF16),
                        pltpu.VMEM((tm, d_mix - D_A), BF16)],
        compiler_params=_params("parallel"),
        name="out_proj",
    )(*args)


def _ffn_kernel(*refs, tm, seq_len, halo, final_norm):
    x_ref = refs[0]
    refs = refs[1:]
    if halo:
        xp_ref, xn_ref = refs[:2]
        refs = refs[2:]
    (mod_ref, g2_ref, wg_ref, wv_ref, cwg_ref, cwv_ref, cbg_ref, cbv_ref, wdn_ref, fg_ref,
     y_ref, hs, zg, zv) = refs
    i = pl.program_id(0)
    j = pl.program_id(1)
    nj = pl.num_programs(1)
    mod = mod_ref[0]
    pad = HALO if halo else SUBLANE
    tf = zg.shape[1]

    def prologue():
        g2 = g2_ref[...]
        scale = 1.0 + mod[4:5]
        shift = mod[3:4]
        x = x_ref[...]
        if halo:
            tiles_per_seq = seq_len // tm
            ti = i % tiles_per_seq
            prev_ok = jnp.where(ti > 0, 1.0, 0.0)
            next_ok = jnp.where(ti < tiles_per_seq - 1, 1.0, 0.0)
            hs[0:HALO, :] = ((_rmsnorm(xp_ref[...], g2) * scale + shift) * prev_ok).astype(BF16)
            hs[HALO:HALO + tm, :] = (_rmsnorm(x, g2) * scale + shift).astype(BF16)
            hs[HALO + tm:, :] = ((_rmsnorm(xn_ref[...], g2) * scale + shift) * next_ok).astype(BF16)
        else:
            hs[...] = (_rmsnorm(x, g2) * scale + shift).astype(BF16)
            zeros = jnp.zeros((pad, tf), F32)
            for z in (zg, zv):
                z[0:pad, :] = zeros
                z[pad + tm:, :] = zeros
        y_ref[...] = jnp.zeros_like(y_ref)

    def ff_tile():
        if halo:
            zg[...] = _dot(hs[...], wg_ref[...])
            zv[...] = _dot(hs[...], wv_ref[...])
        else:
            zg[pad:pad + tm, :] = _dot(hs[...], wg_ref[...])
            zv[pad:pad + tm, :] = _dot(hs[...], wv_ref[...])
            pos = lax.rem(lax.broadcasted_iota(jnp.int32, (tm, 1), 0), seq_len)
            not_first = pos != 0
            not_last = pos != seq_len - 1

        def conv(z, cw_ref, cb_ref):
            cw = cw_ref[...]
            zp = z[pad - 1:pad - 1 + tm, :]
            zn = z[pad + 1:pad + 1 + tm, :]
            if not halo:
                zp = jnp.where(not_first, zp, 0.0)
                zn = jnp.where(not_last, zn, 0.0)
            return cb_ref[...] + zp * cw[0:1, :] + z[pad:pad + tm, :] * cw[1:2, :] + zn * cw[2:3, :]

        g = conv(zg, cwg_ref, cbg_ref)
        act = (g * _sigmoid(g) * conv(zv, cwv_ref, cbv_ref)).astype(BF16)
        y_ref[...] += _dot(act, wdn_ref[...])

    @pl.when(j == 0)
    def _():
        prologue()
        ff_tile()

    @pl.when(j > 0)
    def _():
        ff_tile()

    @pl.when(j == nj - 1)
    def _():
        y = x_ref[...] + mod[5:6] * y_ref[...]
        if final_norm:
            y = _rmsnorm(y, fg_ref[...])
        y_ref[...] = y


def _ffn(x, mod, w, l, final_g, *, seq_len, tm, tf, final_norm):
    t = x.shape[0]
    halo = tm < seq_len
    assert (seq_len % tm == 0) if halo else (tm % seq_len == 0)
    nf = D_FF // tf
    hb = tm // HALO
    n_hb = t // HALO
    n_mod = mod.shape[0]
    pad = HALO if halo else SUBLANE

    def mod_idx(i, j):
        return (jnp.minimum(i * tm // seq_len, n_mod - 1), 0, 0)

    def lspec(shape, idx):
        return pl.BlockSpec((None,) + shape, lambda i, j: (l,) + idx(i, j))

    in_specs = [pl.BlockSpec((tm, D_MODEL), lambda i, j: (i, 0))]
    args = [x]
    if halo:
        in_specs += [
            pl.BlockSpec((HALO, D_MODEL), lambda i, j: (jnp.maximum(i * hb - 1, 0), 0)),
            pl.BlockSpec((HALO, D_MODEL), lambda i, j: (jnp.minimum((i + 1) * hb, n_hb - 1), 0)),
        ]
        args += [x, x]
    in_specs += [
        pl.BlockSpec((1, 6, D_MODEL), mod_idx),
        _layer_spec((1, D_MODEL), l),
        lspec((D_MODEL, tf), lambda i, j: (0, j)),
        lspec((D_MODEL, tf), lambda i, j: (0, nf + j)),
        lspec((FFN_K, tf), lambda i, j: (0, j)),
        lspec((FFN_K, tf), lambda i, j: (0, nf + j)),
        lspec((1, tf), lambda i, j: (0, j)),
        lspec((1, tf), lambda i, j: (0, nf + j)),
        lspec((tf, D_MODEL), lambda i, j: (j, 0)),
        pl.BlockSpec((1, D_MODEL), lambda i, j: (0, 0)),
    ]
    args += [mod, w["norm2_g"], w["w_up"], w["w_up"], w["ffn_conv_w"], w["ffn_conv_w"],
             w["ffn_conv_b"], w["ffn_conv_b"], w["w_down"], final_g]
    h_rows = tm + 2 * HALO if halo else tm
    return pl.pallas_call(
        functools.partial(_ffn_kernel, tm=tm, seq_len=seq_len, halo=halo, final_norm=final_norm),
        out_shape=jax.ShapeDtypeStruct((t, D_MODEL), F32),
        grid=(t // tm, nf),
        in_specs=in_specs,
        out_specs=pl.BlockSpec((tm, D_MODEL), lambda i, j: (i, 0)),
        scratch_shapes=[pltpu.VMEM((h_rows, D_MODEL), BF16),
                        pltpu.VMEM((tm + 2 * pad, tf), F32),
                        pltpu.VMEM((tm + 2 * pad, tf), F32)],
        compiler_params=_params("parallel", "arbitrary"),
        name="conv_ffn",
    )(*args)


def _rope_tables(length):
    pos = np.arange(length)
    r = (pos // GRID_W).astype(np.float64)
    col = (pos % GRID_W).astype(np.float64)
    half = ROPE_DIM // 4
    freq = 1.0 / (ROPE_THETA ** (np.arange(half, dtype=np.float64) / half))
    ar = r[:, None] * freq
    ac = col[:, None] * freq
    zeros = np.zeros((length, LANE - ROPE_DIM))
    cos = np.concatenate([np.cos(ar), np.cos(ar), np.cos(ac), np.cos(ac), zeros], axis=1)
    sin = np.concatenate([-np.sin(ar), np.sin(ar), -np.sin(ac), np.sin(ac), zeros], axis=1)
    return jnp.asarray(cos, F32), jnp.asarray(sin, F32)


def _swap_rope_cols(w):
    q = ROPE_DIM // 4
    return jnp.concatenate([w[..., q:2 * q], w[..., 0:q], w[..., 3 * q:4 * q], w[..., 2 * q:3 * q]], axis=-1)


def _pack_w_in_kernel(w_ref, sel_ref, o_ref):
    o_ref[:, 0:C_ZKR] = w_ref[:, 0:C_ZKR].astype(BF16)
    zkr_w = w_ref[:, C_ZKR:C_ZKR + ROPE_DIM].astype(BF16)
    o_ref[:, C_ZKR:] = _dot(zkr_w, sel_ref[...]).astype(BF16)


def _pack_w_in(w_in):
    q = ROPE_DIM // 4
    partner = np.concatenate([np.arange(q, 2 * q), np.arange(0, q), np.arange(3 * q, 4 * q), np.arange(2 * q, 3 * q)])
    sel = np.zeros((ROPE_DIM, 2 * LANE), np.float32)
    sel[np.arange(ROPE_DIM), np.arange(ROPE_DIM)] = 1.0
    sel[partner, LANE + np.arange(ROPE_DIM)] = 1.0
    rows = 512
    n_in = w_in.shape[-1]
    return pl.pallas_call(
        _pack_w_in_kernel,
        out_shape=jax.ShapeDtypeStruct((DEPTH, D_MODEL, N_IN_EXT), BF16),
        grid=(DEPTH, D_MODEL // rows),
        in_specs=[pl.BlockSpec((None, rows, n_in), lambda l, i: (l, i, 0)),
                  pl.BlockSpec((ROPE_DIM, 2 * LANE), lambda l, i: (0, 0))],
        out_specs=pl.BlockSpec((None, rows, N_IN_EXT), lambda l, i: (l, i, 0)),
        compiler_params=_params("parallel", "parallel"),
        name="pack_w_in",
    )(w_in, jnp.asarray(sel, BF16))


def _prep_weights(p):

    wuq = p["w_uq"].reshape(DEPTH, Q_RANK, N_HEADS, NOPE_DIM + ROPE_DIM)
    nope = wuq[..., :NOPE_DIM].reshape(DEPTH, Q_RANK, HW)
    rope = wuq[..., NOPE_DIM:]
    rope_p = jnp.concatenate([rope, _swap_rope_cols(rope)], axis=-1).reshape(DEPTH, Q_RANK, HW)

    wukv = p["w_ukv"].reshape(DEPTH, KV_RANK, N_HEADS, NOPE_DIM + V_DIM)
    wukv = jnp.concatenate([wukv[..., :NOPE_DIM].reshape(DEPTH, KV_RANK, HW),
                            wukv[..., NOPE_DIM:].reshape(DEPTH, KV_RANK, HW)], axis=-1)

    def vec(name):
        return p[name][:, None, :]

    return {
        "norm1_g": vec("norm1_g"),
        "w_in": _pack_w_in(p["w_in"]),
        "sg_ln_g": vec("sg_ln_g"), "sg_ln_b": vec("sg_ln_b"),
        "sg_w": p["sg_w"].astype(BF16),
        "sg_bias": jnp.broadcast_to(p["sg_b"][..., None], (DEPTH, G_B, CHUNK, LANE)),
        "q_norm_g": vec("q_norm_g"),
        "w_uq": jnp.concatenate([nope, rope_p], axis=-1).astype(BF16),
        "kv_norm_g": vec("kv_norm_g"),
        "w_ukv": wukv.astype(BF16),
        "conv_w": p["conv_w"], "conv_b": vec("conv_b"),
        "conv_ln_g": vec("conv_ln_g"), "conv_ln_b": vec("conv_ln_b"),
        "w_out": p["w_out"].astype(BF16),
        "norm2_g": vec("norm2_g"),
        "w_up": p["ffn_w_up"].astype(BF16),
        "ffn_conv_w": p["ffn_conv_w"], "ffn_conv_b": vec("ffn_conv_b"),
        "w_down": p["ffn_w_down"].astype(BF16),
    }


TM_IN = 512
TM_OUT = 512
TM_FFN = 1024
TF_FFN = 512
TQ_LAT = 1024
KC_LAT = 1024


def kernel(x_prompt, x_sample, cache_ckv, cache_krope, c, c_ctx, norm1_g, w_ada, b_ada, w_in, conv_w, conv_b, conv_ln_g, conv_ln_b, sg_ln_g, sg_ln_b, sg_w, sg_b, q_norm_g, w_uq, kv_norm_g, w_ukv, w_out, norm2_g, ffn_w_up, ffn_conv_w, ffn_conv_b, ffn_w_down, final_g):
    w = _prep_weights(dict(
        norm1_g=norm1_g, w_in=w_in, conv_w=conv_w, conv_b=conv_b, conv_ln_g=conv_ln_g,
        conv_ln_b=conv_ln_b, sg_ln_g=sg_ln_g, sg_ln_b=sg_ln_b, sg_w=sg_w, sg_b=sg_b,
        q_norm_g=q_norm_g, w_uq=w_uq, kv_norm_g=kv_norm_g, w_ukv=w_ukv, w_out=w_out,
        norm2_g=norm2_g, ffn_w_up=ffn_w_up, ffn_conv_w=ffn_conv_w, ffn_conv_b=ffn_conv_b,
        ffn_w_down=ffn_w_down))
    bp, lp_, _ = x_prompt.shape
    bs, ls, _ = x_sample.shape

    cvec = jnp.concatenate([c_ctx[None], c, jnp.zeros((8 - 1 - bs, D_MODEL), F32)], axis=0)
    mod_all = _ada_mod(cvec, w_ada, b_ada).reshape(DEPTH, 8, 6, D_MODEL)
    rope = _rope_tables(ls)
    fg = final_g[None]

    xp = x_prompt.reshape(bp * lp_, D_MODEL)
    xs = x_sample.reshape(bs * ls, D_MODEL)
    ckv_out, kr_out = [], []
    for l in range(DEPTH):
        last = l == DEPTH - 1
        mod_ctx = mod_all[l, 0:1]
        mod_lat = mod_all[l, 1:1 + bs]

        a, b, q, k, v, ckv_n, kr = _in_proj(xp, mod_ctx, w, l, seq_len=lp_, rope=None, emit_cache=True, tm=TM_IN)
        ckv_out.append(ckv_n.reshape(bp, lp_, KV_RANK))
        kr_out.append(kr.reshape(bp, lp_, ROPE_DIM))
        o = _attention(q, k, v, None, batch=bp, seq_len=lp_, tq=lp_, kc=lp_, heads_per_step=N_HEADS)
        xp = _out_proj(xp, mod_ctx, a, b, o, w, l, seq_len=lp_, tm=TM_OUT)
        xp = _ffn(xp, mod_ctx, w, l, fg, seq_len=lp_, tm=TM_FFN, tf=TF_FFN, final_norm=last)

        ctx = _ctx_kv(cache_ckv, cache_krope, w, l)
        a, b, q, k, v = _in_proj(xs, mod_lat, w, l, seq_len=ls, rope=rope, emit_cache=False, tm=TM_IN)
        o = _attention(q, k, v, ctx, batch=bs, seq_len=ls, tq=TQ_LAT, kc=KC_LAT, heads_per_step=4)
        xs = _out_proj(xs, mod_lat, a, b, o, w, l, seq_len=ls, tm=TM_OUT)
        xs = _ffn(xs, mod_lat, w, l, fg, seq_len=ls, tm=TM_FFN, tf=TF_FFN, final_norm=last)

    return (xp.reshape(bp, lp_, D_MODEL), xs.reshape(bs, ls, D_MODEL),
            jnp.stack(ckv_out, axis=1), jnp.stack(kr_out, axis=1))
```

```python
import functools

import jax
import jax.numpy as jnp
import numpy as np
from jax import lax
from jax.experimental import pallas as pl
from jax.experimental.pallas import tpu as pltpu

D_MODEL = 2048
DEPTH = 2
GRID_W = 64
D_A = 512
G_A = 4
CONV_K = 31
D_B = 512
G_B = 4
CHUNK = 128
NOPE_DIM = 128
ROPE_DIM = 64
V_DIM = 128
N_HEADS = 8
Q_RANK = 512
KV_RANK = 256
D_FF = 5632
FFN_K = 3
ROPE_THETA = 10000.0
EPS = 1e-6
ATTN_SCALE = (NOPE_DIM + ROPE_DIM) ** -0.5

LANE = 128
SUBLANE = 8
QK_DIM = 2 * LANE
VO_DIM = 2 * LANE
Q_SCALE = ATTN_SCALE * 1.4426950408889634
HALO = 16
VMEM_LIMIT = 62 * 1024 * 1024

C_ZA = 0
C_ZB = 2 * D_A
C_ZQ = C_ZB + 2 * D_B
C_ZKV = C_ZQ + Q_RANK
C_ZKR = C_ZKV + KV_RANK
N_IN_EXT = C_ZKR + 2 * LANE
HW = N_HEADS * LANE

F32 = jnp.float32
BF16 = jnp.bfloat16


def _dot(a, b):
    return jnp.dot(a, b, preferred_element_type=F32)


def _rmsnorm(x, g):
    return x * lax.rsqrt(jnp.mean(x * x, axis=-1, keepdims=True) + EPS) * g


def _group_ln(x, g, b, groups):
    outs = []
    for i in range(groups):
        xg = x[:, i * LANE:(i + 1) * LANE]
        mu = jnp.mean(xg, axis=-1, keepdims=True)
        xc = xg - mu
        var = jnp.mean(xc * xc, axis=-1, keepdims=True)
        outs.append(xc * lax.rsqrt(var + EPS))
    return jnp.concatenate(outs, axis=-1) * g + b


def _sigmoid(x):
    return 1.0 / (1.0 + jnp.exp(-x))


def _layer_spec(shape, l):
    nd = len(shape)
    return pl.BlockSpec((None,) + tuple(shape), lambda *_: (l,) + (0,) * nd, pipeline_mode=pl.Buffered(1))


def _params(*sem):
    return pltpu.CompilerParams(dimension_semantics=sem, vmem_limit_bytes=VMEM_LIMIT)


def _ada_kernel(c_ref, w_ref, b_ref, o_ref):
    c = c_ref[...]
    s = (c * _sigmoid(c)).astype(BF16)
    o_ref[0] = _dot(s, w_ref[0].astype(BF16)) + b_ref[0]


def _ada_mod(cvec, w_ada, b_ada):
    tn = 1024
    n = 6 * D_MODEL
    return pl.pallas_call(
        _ada_kernel,
        out_shape=jax.ShapeDtypeStruct((DEPTH, 8, n), F32),
        grid=(DEPTH, n // tn),
        in_specs=[
            pl.BlockSpec((8, D_MODEL), lambda l, j: (0, 0)),
            pl.BlockSpec((1, D_MODEL, tn), lambda l, j: (l, 0, j)),
            pl.BlockSpec((1, 1, tn), lambda l, j: (l, 0, j)),
        ],
        out_specs=pl.BlockSpec((1, 8, tn), lambda l, j: (l, 0, j)),
        compiler_params=_params("arbitrary", "arbitrary"),
        name="ada_mod",
    )(cvec, w_ada, b_ada.reshape(DEPTH, 1, n))


def _in_proj_kernel(*refs, tm, use_rope, emit_cache):
    (x_ref, mod_ref, g1_ref, win_ref, sgg_ref, sgb_ref, sgw_ref, sgbias_ref,
     qg_ref, wuq_ref, kvg_ref, wukv_ref) = refs[:12]
    refs = refs[12:]
    if use_rope:
        cos_ref, sin_ref = refs[:2]
        refs = refs[2:]
    a_ref, b_ref, q_ref, k_ref, v_ref = refs[:5]
    refs = refs[5:]
    if emit_cache:
        ckv_ref, kr_ref = refs

    mod = mod_ref[0]
    h = _rmsnorm(x_ref[...], g1_ref[...]) * (1.0 + mod[1:2]) + mod[0:1]
    h = h.astype(BF16)

    za = _dot(h, win_ref[:, C_ZA:C_ZA + 2 * D_A])
    a_ref[...] = (za[:, :D_A] * _sigmoid(za[:, D_A:])).astype(BF16)

    zb = _dot(h, win_ref[:, C_ZB:C_ZB + 2 * D_B])
    zb = 0.5 * zb * (1.0 + lax.erf(zb * (2.0 ** -0.5)))
    u = zb[:, :D_B]
    vn = _group_ln(zb[:, D_B:], sgg_ref[...], sgb_ref[...], G_B).astype(BF16)
    for c in range(tm // CHUNK):
        rows = slice(c * CHUNK, (c + 1) * CHUNK)
        for g in range(G_B):
            cols = slice(g * LANE, (g + 1) * LANE)
            sv = _dot(sgw_ref[g], vn[rows, cols]) + sgbias_ref[g]
            b_ref[rows, cols] = (u[rows, cols] * sv).astype(BF16)

    zq = _dot(h, win_ref[:, C_ZQ:C_ZQ + Q_RANK])
    qn = _rmsnorm(zq, qg_ref[...]).astype(BF16)
    q = _dot(qn, wuq_ref[...])
    if use_rope:
        cos = cos_ref[...]
        sin = sin_ref[...]
    else:
        rope_lanes = lax.broadcasted_iota(jnp.int32, (1, LANE), 1) < ROPE_DIM
    for hd in range(N_HEADS):
        cols = slice(hd * LANE, (hd + 1) * LANE)
        q_ref[hd, :, 0:LANE] = (q[:, cols] * Q_SCALE).astype(BF16)
        qr = q[:, HW:2 * HW][:, cols]
        if use_rope:
            qr = qr * cos + pltpu.roll(qr, ROPE_DIM, axis=1) * sin
        else:
            qr = jnp.where(rope_lanes, qr, 0.0)
        q_ref[hd, :, LANE:QK_DIM] = (qr * Q_SCALE).astype(BF16)

    nkv = KV_RANK + (2 * LANE if use_rope else LANE)
    zkv = _dot(h, win_ref[:, C_ZKV:C_ZKV + nkv])
    ckv_n = _rmsnorm(zkv[:, :KV_RANK], kvg_ref[...])
    kr = zkv[:, KV_RANK:KV_RANK + LANE]
    if emit_cache:
        ckv_ref[...] = ckv_n
        kr_ref[...] = kr[:, :ROPE_DIM]
    if use_rope:
        kr = kr * cos + zkv[:, KV_RANK + LANE:KV_RANK + 2 * LANE] * sin
    kr = kr.astype(BF16)
    kv = _dot(ckv_n.astype(BF16), wukv_ref[...])
    ones = jnp.ones((tm, VO_DIM - V_DIM), BF16)
    for hd in range(N_HEADS):
        cols = slice(hd * LANE, (hd + 1) * LANE)
        k_ref[hd, :, 0:LANE] = kv[:, cols].astype(BF16)
        k_ref[hd, :, LANE:QK_DIM] = kr
        v_ref[hd, :, 0:V_DIM] = kv[:, HW:2 * HW][:, cols].astype(BF16)
        v_ref[hd, :, V_DIM:VO_DIM] = ones


def _in_proj(x, mod, w, l, *, seq_len, rope, emit_cache, tm):
    t = x.shape[0]
    use_rope = rope is not None
    tiles_per_seq = max(seq_len // tm, 1)
    n_mod = mod.shape[0]

    def mod_idx(i):
        return (jnp.minimum(i * tm // seq_len, n_mod - 1), 0, 0)

    in_specs = [
        pl.BlockSpec((tm, D_MODEL), lambda i: (i, 0)),
        pl.BlockSpec((1, 6, D_MODEL), mod_idx),
        _layer_spec((1, D_MODEL), l),
        _layer_spec((D_MODEL, N_IN_EXT), l),
        _layer_spec((1, D_B), l), _layer_spec((1, D_B), l),
        _layer_spec((G_B, CHUNK, CHUNK), l), _layer_spec((G_B, CHUNK, LANE), l),
        _layer_spec((1, Q_RANK), l),
        _layer_spec((Q_RANK, 2 * HW), l),
        _layer_spec((1, KV_RANK), l),
        _layer_spec((KV_RANK, 2 * HW), l),
    ]
    args = [x, mod, w["norm1_g"], w["w_in"], w["sg_ln_g"], w["sg_ln_b"], w["sg_w"], w["sg_bias"],
            w["q_norm_g"], w["w_uq"], w["kv_norm_g"], w["w_ukv"]]
    if use_rope:
        in_specs += [pl.BlockSpec((tm, LANE), lambda i: (i % tiles_per_seq, 0))] * 2
        args += list(rope)
    out_shape = [
        jax.ShapeDtypeStruct((t, D_A), BF16),
        jax.ShapeDtypeStruct((t, D_B), BF16),
        jax.ShapeDtypeStruct((N_HEADS, t, QK_DIM), BF16),
        jax.ShapeDtypeStruct((N_HEADS, t, QK_DIM), BF16),
        jax.ShapeDtypeStruct((N_HEADS, t, VO_DIM), BF16),
    ]
    out_specs = [
        pl.BlockSpec((tm, D_A), lambda i: (i, 0)),
        pl.BlockSpec((tm, D_B), lambda i: (i, 0)),
        pl.BlockSpec((N_HEADS, tm, QK_DIM), lambda i: (0, i, 0)),
        pl.BlockSpec((N_HEADS, tm, QK_DIM), lambda i: (0, i, 0)),
        pl.BlockSpec((N_HEADS, tm, VO_DIM), lambda i: (0, i, 0)),
    ]
    if emit_cache:
        out_shape += [jax.ShapeDtypeStruct((t, KV_RANK), F32), jax.ShapeDtypeStruct((t, ROPE_DIM), F32)]
        out_specs += [pl.BlockSpec((tm, KV_RANK), lambda i: (i, 0)),
                      pl.BlockSpec((tm, ROPE_DIM), lambda i: (i, 0))]
    return pl.pallas_call(
        functools.partial(_in_proj_kernel, tm=tm, use_rope=use_rope, emit_cache=emit_cache),
        out_shape=out_shape,
        grid=(t // tm,),
        in_specs=in_specs,
        out_specs=out_specs,
        compiler_params=_params("parallel"),
        name="in_proj",
    )(*args)


def _ctx_kv_kernel(ckv_ref, kr_ref, wukv_ref, k_ref, v_ref):
    kv = _dot(ckv_ref[...].astype(BF16), wukv_ref[...])
    kr = kr_ref[...].astype(BF16)
    zeros = jnp.zeros((kr.shape[0], LANE - ROPE_DIM), BF16)
    ones = jnp.ones((kr.shape[0], VO_DIM - V_DIM), BF16)
    for hd in range(N_HEADS):
        cols = slice(hd * LANE, (hd + 1) * LANE)
        k_ref[hd, :, 0:LANE] = kv[:, cols].astype(BF16)
        k_ref[hd, :, LANE:LANE + ROPE_DIM] = kr
        k_ref[hd, :, LANE + ROPE_DIM:QK_DIM] = zeros
        v_ref[hd, :, 0:V_DIM] = kv[:, HW:2 * HW][:, cols].astype(BF16)
        v_ref[hd, :, V_DIM:VO_DIM] = ones


def _ctx_kv(cache_ckv, cache_krope, w, l):
    bsz, _, past, _ = cache_ckv.shape
    return pl.pallas_call(
        _ctx_kv_kernel,
        out_shape=[jax.ShapeDtypeStruct((N_HEADS, bsz * past, QK_DIM), BF16),
                   jax.ShapeDtypeStruct((N_HEADS, bsz * past, VO_DIM), BF16)],
        grid=(bsz,),
        in_specs=[pl.BlockSpec((None, None, past, KV_RANK), lambda b: (b, l, 0, 0)),
                  pl.BlockSpec((None, None, past, ROPE_DIM), lambda b: (b, l, 0, 0)),
                  _layer_spec((KV_RANK, 2 * HW), l)],
        out_specs=[pl.BlockSpec((N_HEADS, past, QK_DIM), lambda b: (0, b, 0)),
                   pl.BlockSpec((N_HEADS, past, VO_DIM), lambda b: (0, b, 0))],
        compiler_params=_params("parallel"),
        name="ctx_kv",
    )(cache_ckv, cache_krope, w["w_ukv"])


def _attn_kernel(*refs, heads, has_ctx, kc):
    if has_ctx:
        q_ref, kl_ref, vl_ref, kx_ref, vx_ref, o_ref = refs
    else:
        q_ref, kl_ref, vl_ref, o_ref = refs
    for hd in range(heads):
        q = q_ref[hd]
        chunks = []
        if has_ctx:
            chunks.append((kx_ref, vx_ref, 0, kx_ref.shape[1]))
        n_keys = kl_ref.shape[1]
        for s0 in range(0, n_keys, kc):
            chunks.append((kl_ref, vl_ref, s0, min(kc, n_keys - s0)))
        m = pv = None
        for (k_r, v_r, s0, n) in chunks:
            k = k_r[hd, s0:s0 + n, :]
            v = v_r[hd, s0:s0 + n, :]
            s = lax.dot_general(q, k, (((1,), (1,)), ((), ())), preferred_element_type=F32)
            m_c = jnp.max(s, axis=-1, keepdims=True)
            if m is None:
                m = m_c
                pv = _dot(jnp.exp2(s - m).astype(BF16), v)
            else:
                m_new = jnp.maximum(m, m_c)
                pv = jnp.exp2(m - m_new) * pv + _dot(jnp.exp2(s - m_new).astype(BF16), v)
                m = m_new
        o_ref[:, hd * V_DIM:(hd + 1) * V_DIM] = (pv[:, :V_DIM] / pv[:, V_DIM:]).astype(o_ref.dtype)


def _attention(q, k, v, ctx, *, batch, seq_len, tq, kc, heads_per_step):
    t = q.shape[1]
    nq = seq_len // tq
    hb = heads_per_step
    has_ctx = ctx is not None
    in_specs = [
        pl.BlockSpec((hb, tq, QK_DIM), lambda b, h, i: (h, b * nq + i, 0)),
        pl.BlockSpec((hb, seq_len, QK_DIM), lambda b, h, i: (h, b, 0)),
        pl.BlockSpec((hb, seq_len, VO_DIM), lambda b, h, i: (h, b, 0)),
    ]
    args = [q, k, v]
    if has_ctx:
        past = ctx[0].shape[1] // batch
        in_specs += [pl.BlockSpec((hb, past, QK_DIM), lambda b, h, i: (h, b, 0)),
                     pl.BlockSpec((hb, past, VO_DIM), lambda b, h, i: (h, b, 0))]
        args += list(ctx)
    return pl.pallas_call(
        functools.partial(_attn_kernel, heads=hb, has_ctx=has_ctx, kc=kc),
        out_shape=jax.ShapeDtypeStruct((t, N_HEADS * V_DIM), BF16),
        grid=(batch, N_HEADS // hb, nq),
        in_specs=in_specs,
        out_specs=pl.BlockSpec((tq, hb * V_DIM), lambda b, h, i: (b * nq + i, h)),
        compiler_params=_params("parallel", "parallel", "arbitrary"),
        name="attention",
    )(*args)


CONV_ROWS = 64


def _conv_copy_rows(seg):
    return seg + (HALO + CONV_K // 2) // SUBLANE * SUBLANE


def _out_proj_kernel(*refs, tm, seg, halo, tiles_per_seq):
    x_ref, mod_ref, a_ref = refs[:3]
    refs = refs[3:]
    if halo:
        ap_ref, an_ref = refs[:2]
        refs = refs[2:]
    b_ref, o_ref, cw_ref, cb_ref, lg_ref, lb_ref, wout_ref, y_ref, abuf, sh, aout, cat = refs

    cw = cw_ref[...]
    ext = _conv_copy_rows(seg)
    for sgm in range(tm // seg):
        r0 = sgm * seg
        if halo:
            ti = pl.program_id(0) % tiles_per_seq
            abuf[0:HALO, :] = ap_ref[...].astype(F32) * jnp.where(ti > 0, 1.0, 0.0)
            abuf[HALO + seg:, :] = an_ref[...].astype(F32) * jnp.where(ti < tiles_per_seq - 1, 1.0, 0.0)
        else:
            abuf[0:HALO, :] = jnp.zeros((HALO, D_A), F32)
            abuf[HALO + seg:, :] = jnp.zeros((HALO, D_A), F32)
        abuf[HALO:HALO + seg, :] = a_ref[r0:r0 + seg, :].astype(F32)
        for r in range(1, SUBLANE):
            sh[r - 1, :, :] = abuf[r:r + ext, :]
        for t0 in range(0, seg, CONV_ROWS):
            acc = jnp.zeros((CONV_ROWS, D_A), F32) + cb_ref[...]
            for kk in range(CONV_K):
                q8, r = divmod(HALO - CONV_K // 2 + kk, SUBLANE)
                rows = slice(t0 + q8 * SUBLANE, t0 + q8 * SUBLANE + CONV_ROWS)
                src = abuf[rows, :] if r == 0 else sh[r - 1, rows, :]
                acc = acc + src * cw[kk:kk + 1, :]
            an = _group_ln(acc, lg_ref[...], lb_ref[...], G_A)
            aout[r0 + t0:r0 + t0 + CONV_ROWS, :] = (an * _sigmoid(an)).astype(BF16)

    gate = mod_ref[0][2:3]
    cat[:, 0:D_B] = b_ref[...]
    cat[:, D_B:] = o_ref[...]
    y_ref[...] = x_ref[...] + gate * _dot(cat[...], wout_ref[D_A:, :])
    y_ref[...] += gate * _dot(aout[...], wout_ref[0:D_A, :])


def _out_proj(x, mod, a, b, o, w, l, *, seq_len, tm):
    t = x.shape[0]
    halo = tm < seq_len
    assert (seq_len % tm == 0) if halo else (tm % seq_len == 0)
    seg = tm if halo else seq_len
    tiles_per_seq = max(seq_len // tm, 1)
    hb = tm // HALO
    n_hb = t // HALO
    n_mod = mod.shape[0]

    def mod_idx(i):
        return (jnp.minimum(i * tm // seq_len, n_mod - 1), 0, 0)

    d_mix = D_A + D_B + N_HEADS * V_DIM
    in_specs = [
        pl.BlockSpec((tm, D_MODEL), lambda i: (i, 0)),
        pl.BlockSpec((1, 6, D_MODEL), mod_idx),
        pl.BlockSpec((tm, D_A), lambda i: (i, 0)),
    ]
    args = [x, mod, a]
    if halo:
        in_specs += [pl.BlockSpec((HALO, D_A), lambda i: (jnp.maximum(i * hb - 1, 0), 0)),
                     pl.BlockSpec((HALO, D_A), lambda i: (jnp.minimum((i + 1) * hb, n_hb - 1), 0))]
        args += [a, a]
    in_specs += [
        pl.BlockSpec((tm, D_B), lambda i: (i, 0)),
        pl.BlockSpec((tm, N_HEADS * V_DIM), lambda i: (i, 0)),
        _layer_spec((CONV_K, D_A), l), _layer_spec((1, D_A), l), _layer_spec((1, D_A), l),
        _layer_spec((1, D_A), l),
        _layer_spec((d_mix, D_MODEL), l),
    ]
    args += [b, o, w["conv_w"], w["conv_b"], w["conv_ln_g"], w["conv_ln_b"], w["w_out"]]
    return pl.pallas_call(
        functools.partial(_out_proj_kernel, tm=tm, seg=seg, halo=halo, tiles_per_seq=tiles_per_seq),
        out_shape=jax.ShapeDtypeStruct((t, D_MODEL), F32),
        grid=(t // tm,),
        in_specs=in_specs,
        out_specs=pl.BlockSpec((tm, D_MODEL), lambda i: (i, 0)),
        scratch_shapes=[pltpu.VMEM((seg + 2 * HALO, D_A), F32),
                        pltpu.VMEM((SUBLANE - 1, _conv_copy_rows(seg), D_A), F32),
                        pltpu.VMEM((tm, D_A), BF16),
                        pltpu.VMEM((tm, d_mix - D_A), BF16)],
        compiler_params=_params("parallel"),
        name="out_proj",
    )(*args)


def _ffn_kernel(*refs, tm, seq_len, halo, final_norm):
    x_ref = refs[0]
    refs = refs[1:]
    if halo:
        xp_ref, xn_ref = refs[:2]
        refs = refs[2:]
    (mod_ref, g2_ref, wg_ref, wv_ref, cwg_ref, cwv_ref, cbg_ref, cbv_ref, wdn_ref, fg_ref,
     y_ref, hs, zg, zv) = refs
    i = pl.program_id(0)
    j = pl.program_id(1)
    nj = pl.num_programs(1)
    mod = mod_ref[0]
    pad = HALO if halo else SUBLANE
    tf = zg.shape[1]

    def prologue():
        g2 = g2_ref[...]
        scale = 1.0 + mod[4:5]
        shift = mod[3:4]
        x = x_ref[...]
        if halo:
            tiles_per_seq = seq_len // tm
            ti = i % tiles_per_seq
            prev_ok = jnp.where(ti > 0, 1.0, 0.0)
            next_ok = jnp.where(ti < tiles_per_seq - 1, 1.0, 0.0)
            hs[0:HALO, :] = ((_rmsnorm(xp_ref[...], g2) * scale + shift) * prev_ok).astype(BF16)
            hs[HALO:HALO + tm, :] = (_rmsnorm(x, g2) * scale + shift).astype(BF16)
            hs[HALO + tm:, :] = ((_rmsnorm(xn_ref[...], g2) * scale + shift) * next_ok).astype(BF16)
        else:
            hs[...] = (_rmsnorm(x, g2) * scale + shift).astype(BF16)
            zeros = jnp.zeros((pad, tf), F32)
            for z in (zg, zv):
                z[0:pad, :] = zeros
                z[pad + tm:, :] = zeros

    def ff_tile():
        if halo:
            zg[...] = _dot(hs[...], wg_ref[...])
            zv[...] = _dot(hs[...], wv_ref[...])
        else:
            zg[pad:pad + tm, :] = _dot(hs[...], wg_ref[...])
            zv[pad:pad + tm, :] = _dot(hs[...], wv_ref[...])
            pos = lax.rem(lax.broadcasted_iota(jnp.int32, (tm, 1), 0), seq_len)
            not_first = pos != 0
            not_last = pos != seq_len - 1

        def conv(z, cw_ref, cb_ref):
            cw = cw_ref[...]
            zp = z[pad - 1:pad - 1 + tm, :]
            zn = z[pad + 1:pad + 1 + tm, :]
            if not halo:
                zp = jnp.where(not_first, zp, 0.0)
                zn = jnp.where(not_last, zn, 0.0)
            return cb_ref[...] + zp * cw[0:1, :] + z[pad:pad + tm, :] * cw[1:2, :] + zn * cw[2:3, :]

        g = conv(zg, cwg_ref, cbg_ref)
        act = (g * _sigmoid(g) * conv(zv, cwv_ref, cbv_ref)).astype(BF16)
        return _dot(act, wdn_ref[...])

    @pl.when(j == 0)
    def _():
        prologue()
        y_ref[...] = ff_tile()

    @pl.when((j > 0) & (j < nj - 1))
    def _():
        y_ref[...] += ff_tile()

    @pl.when(j == nj - 1)
    def _():
        y = x_ref[...] + mod[5:6] * (y_ref[...] + ff_tile())
        if final_norm:
            y = _rmsnorm(y, fg_ref[...])
        y_ref[...] = y


def _ffn(x, mod, w, l, final_g, *, seq_len, tm, tf, final_norm):
    t = x.shape[0]
    halo = tm < seq_len
    assert (seq_len % tm == 0) if halo else (tm % seq_len == 0)
    nf = D_FF // tf
    assert nf >= 2
    hb = tm // HALO
    n_hb = t // HALO
    n_mod = mod.shape[0]
    pad = HALO if halo else SUBLANE

    def mod_idx(i, j):
        return (jnp.minimum(i * tm // seq_len, n_mod - 1), 0, 0)

    def lspec(shape, idx):
        return pl.BlockSpec((None,) + shape, lambda i, j: (l,) + idx(i, j))

    in_specs = [pl.BlockSpec((tm, D_MODEL), lambda i, j: (i, 0))]
    args = [x]
    if halo:
        in_specs += [
            pl.BlockSpec((HALO, D_MODEL), lambda i, j: (jnp.maximum(i * hb - 1, 0), 0)),
            pl.BlockSpec((HALO, D_MODEL), lambda i, j: (jnp.minimum((i + 1) * hb, n_hb - 1), 0)),
        ]
        args += [x, x]
    in_specs += [
        pl.BlockSpec((1, 6, D_MODEL), mod_idx),
        _layer_spec((1, D_MODEL), l),
        lspec((D_MODEL, tf), lambda i, j: (0, j)),
        lspec((D_MODEL, tf), lambda i, j: (0, nf + j)),
        lspec((FFN_K, tf), lambda i, j: (0, j)),
        lspec((FFN_K, tf), lambda i, j: (0, nf + j)),
        lspec((1, tf), lambda i, j: (0, j)),
        lspec((1, tf), lambda i, j: (0, nf + j)),
        lspec((tf, D_MODEL), lambda i, j: (j, 0)),
        pl.BlockSpec((1, D_MODEL), lambda i, j: (0, 0)),
    ]
    args += [mod, w["norm2_g"], w["w_up"], w["w_up"], w["ffn_conv_w"], w["ffn_conv_w"],
             w["ffn_conv_b"], w["ffn_conv_b"], w["w_down"], final_g]
    h_rows = tm + 2 * HALO if halo else tm
    return pl.pallas_call(
        functools.partial(_ffn_kernel, tm=tm, seq_len=seq_len, halo=halo, final_norm=final_norm),
        out_shape=jax.ShapeDtypeStruct((t, D_MODEL), F32),
        grid=(t // tm, nf),
        in_specs=in_specs,
        out_specs=pl.BlockSpec((tm, D_MODEL), lambda i, j: (i, 0)),
        scratch_shapes=[pltpu.VMEM((h_rows, D_MODEL), BF16),
                        pltpu.VMEM((tm + 2 * pad, tf), F32),
                        pltpu.VMEM((tm + 2 * pad, tf), F32)],
        compiler_params=_params("parallel", "arbitrary"),
        name="conv_ffn",
    )(*args)


def _rope_tables(length):
    pos = np.arange(length)
    r = (pos // GRID_W).astype(np.float64)
    col = (pos % GRID_W).astype(np.float64)
    half = ROPE_DIM // 4
    freq = 1.0 / (ROPE_THETA ** (np.arange(half, dtype=np.float64) / half))
    ar = r[:, None] * freq
    ac = col[:, None] * freq
    zeros = np.zeros((length, LANE - ROPE_DIM))
    cos = np.concatenate([np.cos(ar), np.cos(ar), np.cos(ac), np.cos(ac), zeros], axis=1)
    sin = np.concatenate([-np.sin(ar), np.sin(ar), -np.sin(ac), np.sin(ac), zeros], axis=1)
    return jnp.asarray(cos, F32), jnp.asarray(sin, F32)


def _swap_rope_cols(w):
    q = ROPE_DIM // 4
    return jnp.concatenate([w[..., q:2 * q], w[..., 0:q], w[..., 3 * q:4 * q], w[..., 2 * q:3 * q]], axis=-1)


def _pack_w_in_kernel(wt_ref, sel_ref, o_ref):
    rows = wt_ref.shape[1]
    wt = wt_ref[...].astype(BF16)
    eye = (lax.broadcasted_iota(jnp.int32, (rows, rows), 0)
           == lax.broadcasted_iota(jnp.int32, (rows, rows), 1)).astype(BF16)
    nt = (((1,), (1,)), ((), ()))
    o_ref[:, 0:C_ZKR] = lax.dot_general(eye, wt[0:C_ZKR], nt, preferred_element_type=F32).astype(BF16)
    zkr_w = lax.dot_general(eye, wt[C_ZKR:C_ZKR + ROPE_DIM], nt, preferred_element_type=F32).astype(BF16)
    o_ref[:, C_ZKR:] = _dot(zkr_w, sel_ref[...]).astype(BF16)


def _pack_w_in(w_in):
    q = ROPE_DIM // 4
    partner = np.concatenate([np.arange(q, 2 * q), np.arange(0, q), np.arange(3 * q, 4 * q), np.arange(2 * q, 3 * q)])
    sel = np.zeros((ROPE_DIM, 2 * LANE), np.float32)
    sel[np.arange(ROPE_DIM), np.arange(ROPE_DIM)] = 1.0
    sel[partner, LANE + np.arange(ROPE_DIM)] = 1.0
    rows = 2 * LANE
    n_in = w_in.shape[-1]
    return pl.pallas_call(
        _pack_w_in_kernel,
        out_shape=jax.ShapeDtypeStruct((DEPTH, D_MODEL, N_IN_EXT), BF16),
        grid=(DEPTH, D_MODEL // rows),
        in_specs=[pl.BlockSpec((None, n_in, rows), lambda l, i: (l, 0, i)),
                  pl.BlockSpec((ROPE_DIM, 2 * LANE), lambda l, i: (0, 0))],
        out_specs=pl.BlockSpec((None, rows, N_IN_EXT), lambda l, i: (l, i, 0)),
        compiler_params=_params("parallel", "parallel"),
        name="pack_w_in",
    )(jnp.swapaxes(w_in, 1, 2), jnp.asarray(sel, BF16))


def _prep_weights(p):

    wuq = p["w_uq"].reshape(DEPTH, Q_RANK, N_HEADS, NOPE_DIM + ROPE_DIM)
    nope = wuq[..., :NOPE_DIM].reshape(DEPTH, Q_RANK, HW)
    rope = wuq[..., NOPE_DIM:]
    rope_p = jnp.concatenate([rope, _swap_rope_cols(rope)], axis=-1).reshape(DEPTH, Q_RANK, HW)

    wukv = p["w_ukv"].reshape(DEPTH, KV_RANK, N_HEADS, NOPE_DIM + V_DIM)
    wukv = jnp.concatenate([wukv[..., :NOPE_DIM].reshape(DEPTH, KV_RANK, HW),
                            wukv[..., NOPE_DIM:].reshape(DEPTH, KV_RANK, HW)], axis=-1)

    def vec(name):
        return p[name][:, None, :]

    return {
        "norm1_g": vec("norm1_g"),
        "w_in": _pack_w_in(p["w_in"]),
        "sg_ln_g": vec("sg_ln_g"), "sg_ln_b": vec("sg_ln_b"),
        "sg_w": p["sg_w"].astype(BF16),
        "sg_bias": jnp.broadcast_to(p["sg_b"][..., None], (DEPTH, G_B, CHUNK, LANE)),
        "q_norm_g": vec("q_norm_g"),
        "w_uq": jnp.concatenate([nope, rope_p], axis=-1).astype(BF16),
        "kv_norm_g": vec("kv_norm_g"),
        "w_ukv": wukv.astype(BF16),
        "conv_w": p["conv_w"], "conv_b": vec("conv_b"),
        "conv_ln_g": vec("conv_ln_g"), "conv_ln_b": vec("conv_ln_b"),
        "w_out": p["w_out"].astype(BF16),
        "norm2_g": vec("norm2_g"),
        "w_up": p["ffn_w_up"].astype(BF16),
        "ffn_conv_w": p["ffn_conv_w"], "ffn_conv_b": vec("ffn_conv_b"),
        "w_down": p["ffn_w_down"].astype(BF16),
    }


TM_IN = 512
TM_OUT = 512
TM_FFN = 1024
TF_FFN = 512
TQ_LAT = 1024
KC_LAT = 1024


def kernel(x_prompt, x_sample, cache_ckv, cache_krope, c, c_ctx, norm1_g, w_ada, b_ada, w_in, conv_w, conv_b, conv_ln_g, conv_ln_b, sg_ln_g, sg_ln_b, sg_w, sg_b, q_norm_g, w_uq, kv_norm_g, w_ukv, w_out, norm2_g, ffn_w_up, ffn_conv_w, ffn_conv_b, ffn_w_down, final_g):
    w = _prep_weights(dict(
        norm1_g=norm1_g, w_in=w_in, conv_w=conv_w, conv_b=conv_b, conv_ln_g=conv_ln_g,
        conv_ln_b=conv_ln_b, sg_ln_g=sg_ln_g, sg_ln_b=sg_ln_b, sg_w=sg_w, sg_b=sg_b,
        q_norm_g=q_norm_g, w_uq=w_uq, kv_norm_g=kv_norm_g, w_ukv=w_ukv, w_out=w_out,
        norm2_g=norm2_g, ffn_w_up=ffn_w_up, ffn_conv_w=ffn_conv_w, ffn_conv_b=ffn_conv_b,
        ffn_w_down=ffn_w_down))
    bp, lp_, _ = x_prompt.shape
    bs, ls, _ = x_sample.shape

    cvec = jnp.concatenate([c_ctx[None], c, jnp.zeros((8 - 1 - bs, D_MODEL), F32)], axis=0)
    mod_all = _ada_mod(cvec, w_ada, b_ada).reshape(DEPTH, 8, 6, D_MODEL)
    rope = _rope_tables(ls)
    fg = final_g[None]

    xp = x_prompt.reshape(bp * lp_, D_MODEL)
    xs = x_sample.reshape(bs * ls, D_MODEL)
    ckv_out, kr_out = [], []
    for l in range(DEPTH):
        last = l == DEPTH - 1
        mod_ctx = mod_all[l, 0:1]
        mod_lat = mod_all[l, 1:1 + bs]

        a, b, q, k, v, ckv_n, kr = _in_proj(xp, mod_ctx, w, l, seq_len=lp_, rope=None, emit_cache=True, tm=TM_IN)
        ckv_out.append(ckv_n.reshape(bp, lp_, KV_RANK))
        kr_out.append(kr.reshape(bp, lp_, ROPE_DIM))
        o = _attention(q, k, v, None, batch=bp, seq_len=lp_, tq=lp_, kc=lp_, heads_per_step=N_HEADS)
        xp = _out_proj(xp, mod_ctx, a, b, o, w, l, seq_len=lp_, tm=TM_OUT)
        xp = _ffn(xp, mod_ctx, w, l, fg, seq_len=lp_, tm=TM_FFN, tf=TF_FFN, final_norm=last)

        ctx = _ctx_kv(cache_ckv, cache_krope, w, l)
        a, b, q, k, v = _in_proj(xs, mod_lat, w, l, seq_len=ls, rope=rope, emit_cache=False, tm=TM_IN)
        o = _attention(q, k, v, ctx, batch=bs, seq_len=ls, tq=TQ_LAT, kc=KC_LAT, heads_per_step=4)
        xs = _out_proj(xs, mod_lat, a, b, o, w, l, seq_len=ls, tm=TM_OUT)
        xs = _ffn(xs, mod_lat, w, l, fg, seq_len=ls, tm=TM_FFN, tf=TF_FFN, final_norm=last)

    return (xp.reshape(bp, lp_, D_MODEL), xs.reshape(bs, ls, D_MODEL),
            jnp.stack(ckv_out, axis=1), jnp.stack(kr_out, axis=1))
```

```python
import functools

import jax
import jax.numpy as jnp
import numpy as np
from jax import lax
from jax.experimental import pallas as pl
from jax.experimental.pallas import tpu as pltpu

D_MODEL = 2048
DEPTH = 2
GRID_W = 64
D_A = 512
G_A = 4
CONV_K = 31
D_B = 512
G_B = 4
CHUNK = 128
NOPE_DIM = 128
ROPE_DIM = 64
V_DIM = 128
N_HEADS = 8
Q_RANK = 512
KV_RANK = 256
D_FF = 5632
FFN_K = 3
ROPE_THETA = 10000.0
EPS = 1e-6
ATTN_SCALE = (NOPE_DIM + ROPE_DIM) ** -0.5

LANE = 128
SUBLANE = 8
QK_DIM = 2 * LANE
VO_DIM = 2 * LANE
Q_SCALE = ATTN_SCALE * 1.4426950408889634
HALO = 16
VMEM_LIMIT = 62 * 1024 * 1024

C_ZA = 0
C_ZB = 2 * D_A
C_ZQ = C_ZB + 2 * D_B
C_ZKV = C_ZQ + Q_RANK
C_ZKR = C_ZKV + KV_RANK
N_IN_EXT = C_ZKR + 2 * LANE
HW = N_HEADS * LANE

F32 = jnp.float32
BF16 = jnp.bfloat16


def _dot(a, b):
    return jnp.dot(a, b, preferred_element_type=F32)


def _rmsnorm(x, g):
    return x * lax.rsqrt(jnp.mean(x * x, axis=-1, keepdims=True) + EPS) * g


def _group_ln(x, g, b, groups):
    outs = []
    for i in range(groups):
        xg = x[:, i * LANE:(i + 1) * LANE]
        mu = jnp.mean(xg, axis=-1, keepdims=True)
        xc = xg - mu
        var = jnp.mean(xc * xc, axis=-1, keepdims=True)
        outs.append(xc * lax.rsqrt(var + EPS))
    return jnp.concatenate(outs, axis=-1) * g + b


def _sigmoid(x):
    return 0.5 + 0.5 * jnp.tanh(0.5 * x)


def _layer_spec(shape, l):
    nd = len(shape)
    return pl.BlockSpec((None,) + tuple(shape), lambda *_: (l,) + (0,) * nd, pipeline_mode=pl.Buffered(1))


def _params(*sem):
    return pltpu.CompilerParams(dimension_semantics=sem, vmem_limit_bytes=VMEM_LIMIT)


def _ada_kernel(c_ref, w_ref, b_ref, o_ref):
    c = c_ref[...]
    s = (c * _sigmoid(c)).astype(BF16)
    o_ref[0] = _dot(s, w_ref[0].astype(BF16)) + b_ref[0]


def _ada_mod(cvec, w_ada, b_ada):
    tn = 1024
    n = 6 * D_MODEL
    return pl.pallas_call(
        _ada_kernel,
        out_shape=jax.ShapeDtypeStruct((DEPTH, 8, n), F32),
        grid=(DEPTH, n // tn),
        in_specs=[
            pl.BlockSpec((8, D_MODEL), lambda l, j: (0, 0)),
            pl.BlockSpec((1, D_MODEL, tn), lambda l, j: (l, 0, j)),
            pl.BlockSpec((1, 1, tn), lambda l, j: (l, 0, j)),
        ],
        out_specs=pl.BlockSpec((1, 8, tn), lambda l, j: (l, 0, j)),
        compiler_params=_params("arbitrary", "arbitrary"),
        name="ada_mod",
    )(cvec, w_ada, b_ada.reshape(DEPTH, 1, n))


def _in_proj_kernel(*refs, tm, use_rope, emit_cache):
    (x_ref, mod_ref, g1_ref, win_ref, sgg_ref, sgb_ref, sgw_ref, sgbias_ref,
     qg_ref, wuq_ref, kvg_ref, wukv_ref) = refs[:12]
    refs = refs[12:]
    if use_rope:
        cos_ref, sin_ref = refs[:2]
        refs = refs[2:]
    a_ref, b_ref, q_ref, k_ref, v_ref = refs[:5]
    refs = refs[5:]
    if emit_cache:
        ckv_ref, kr_ref = refs

    mod = mod_ref[0]
    h = _rmsnorm(x_ref[...], g1_ref[...]) * (1.0 + mod[1:2]) + mod[0:1]
    h = h.astype(BF16)

    za = _dot(h, win_ref[:, C_ZA:C_ZA + 2 * D_A])
    a_ref[...] = (za[:, :D_A] * _sigmoid(za[:, D_A:])).astype(BF16)

    zb = _dot(h, win_ref[:, C_ZB:C_ZB + 2 * D_B])
    zb = 0.5 * zb * (1.0 + lax.erf(zb * (2.0 ** -0.5)))
    u = zb[:, :D_B]
    vn = _group_ln(zb[:, D_B:], sgg_ref[...], sgb_ref[...], G_B).astype(BF16)
    for c in range(tm // CHUNK):
        rows = slice(c * CHUNK, (c + 1) * CHUNK)
        for g in range(G_B):
            cols = slice(g * LANE, (g + 1) * LANE)
            sv = _dot(sgw_ref[g], vn[rows, cols]) + sgbias_ref[g]
            b_ref[rows, cols] = (u[rows, cols] * sv).astype(BF16)

    zq = _dot(h, win_ref[:, C_ZQ:C_ZQ + Q_RANK])
    qn = _rmsnorm(zq, qg_ref[...]).astype(BF16)
    q = _dot(qn, wuq_ref[...])
    if use_rope:
        cos = cos_ref[...]
        sin = sin_ref[...]
    else:
        rope_lanes = lax.broadcasted_iota(jnp.int32, (1, LANE), 1) < ROPE_DIM
    for hd in range(N_HEADS):
        cols = slice(hd * LANE, (hd + 1) * LANE)
        q_ref[hd, :, 0:LANE] = (q[:, cols] * Q_SCALE).astype(BF16)
        qr = q[:, HW:2 * HW][:, cols]
        if use_rope:
            qr = qr * cos + pltpu.roll(qr, ROPE_DIM, axis=1) * sin
        else:
            qr = jnp.where(rope_lanes, qr, 0.0)
        q_ref[hd, :, LANE:QK_DIM] = (qr * Q_SCALE).astype(BF16)

    nkv = KV_RANK + (2 * LANE if use_rope else LANE)
    zkv = _dot(h, win_ref[:, C_ZKV:C_ZKV + nkv])
    ckv_n = _rmsnorm(zkv[:, :KV_RANK], kvg_ref[...])
    kr = zkv[:, KV_RANK:KV_RANK + LANE]
    if emit_cache:
        ckv_ref[...] = ckv_n
        kr_ref[...] = kr[:, :ROPE_DIM]
    if use_rope:
        kr = kr * cos + zkv[:, KV_RANK + LANE:KV_RANK + 2 * LANE] * sin
    kr = kr.astype(BF16)
    kv = _dot(ckv_n.astype(BF16), wukv_ref[...])
    ones = jnp.ones((tm, VO_DIM - V_DIM), BF16)
    for hd in range(N_HEADS):
        cols = slice(hd * LANE, (hd + 1) * LANE)
        k_ref[hd, :, 0:LANE] = kv[:, cols].astype(BF16)
        k_ref[hd, :, LANE:QK_DIM] = kr
        v_ref[hd, :, 0:V_DIM] = kv[:, HW:2 * HW][:, cols].astype(BF16)
        v_ref[hd, :, V_DIM:VO_DIM] = ones


def _in_proj(x, mod, w, l, *, seq_len, rope, emit_cache, tm):
    t = x.shape[0]
    use_rope = rope is not None
    tiles_per_seq = max(seq_len // tm, 1)
    n_mod = mod.shape[0]

    def mod_idx(i):
        return (jnp.minimum(i * tm // seq_len, n_mod - 1), 0, 0)

    in_specs = [
        pl.BlockSpec((tm, D_MODEL), lambda i: (i, 0)),
        pl.BlockSpec((1, 6, D_MODEL), mod_idx),
        _layer_spec((1, D_MODEL), l),
        _layer_spec((D_MODEL, N_IN_EXT), l),
        _layer_spec((1, D_B), l), _layer_spec((1, D_B), l),
        _layer_spec((G_B, CHUNK, CHUNK), l), _layer_spec((G_B, CHUNK, LANE), l),
        _layer_spec((1, Q_RANK), l),
        _layer_spec((Q_RANK, 2 * HW), l),
        _layer_spec((1, KV_RANK), l),
        _layer_spec((KV_RANK, 2 * HW), l),
    ]
    args = [x, mod, w["norm1_g"], w["w_in"], w["sg_ln_g"], w["sg_ln_b"], w["sg_w"], w["sg_bias"],
            w["q_norm_g"], w["w_uq"], w["kv_norm_g"], w["w_ukv"]]
    if use_rope:
        in_specs += [pl.BlockSpec((tm, LANE), lambda i: (i % tiles_per_seq, 0))] * 2
        args += list(rope)
    out_shape = [
        jax.ShapeDtypeStruct((t, D_A), BF16),
        jax.ShapeDtypeStruct((t, D_B), BF16),
        jax.ShapeDtypeStruct((N_HEADS, t, QK_DIM), BF16),
        jax.ShapeDtypeStruct((N_HEADS, t, QK_DIM), BF16),
        jax.ShapeDtypeStruct((N_HEADS, t, VO_DIM), BF16),
    ]
    out_specs = [
        pl.BlockSpec((tm, D_A), lambda i: (i, 0)),
        pl.BlockSpec((tm, D_B), lambda i: (i, 0)),
        pl.BlockSpec((N_HEADS, tm, QK_DIM), lambda i: (0, i, 0)),
        pl.BlockSpec((N_HEADS, tm, QK_DIM), lambda i: (0, i, 0)),
        pl.BlockSpec((N_HEADS, tm, VO_DIM), lambda i: (0, i, 0)),
    ]
    if emit_cache:
        out_shape += [jax.ShapeDtypeStruct((t, KV_RANK), F32), jax.ShapeDtypeStruct((t, ROPE_DIM), F32)]
        out_specs += [pl.BlockSpec((tm, KV_RANK), lambda i: (i, 0)),
                      pl.BlockSpec((tm, ROPE_DIM), lambda i: (i, 0))]
    return pl.pallas_call(
        functools.partial(_in_proj_kernel, tm=tm, use_rope=use_rope, emit_cache=emit_cache),
        out_shape=out_shape,
        grid=(t // tm,),
        in_specs=in_specs,
        out_specs=out_specs,
        compiler_params=_params("parallel"),
        name="in_proj",
    )(*args)


def _ctx_kv_kernel(ckv_ref, kr_ref, wukv_ref, k_ref, v_ref):
    kv = _dot(ckv_ref[...].astype(BF16), wukv_ref[...])
    kr = kr_ref[...].astype(BF16)
    zeros = jnp.zeros((kr.shape[0], LANE - ROPE_DIM), BF16)
    ones = jnp.ones((kr.shape[0], VO_DIM - V_DIM), BF16)
    for hd in range(N_HEADS):
        cols = slice(hd * LANE, (hd + 1) * LANE)
        k_ref[hd, :, 0:LANE] = kv[:, cols].astype(BF16)
        k_ref[hd, :, LANE:LANE + ROPE_DIM] = kr
        k_ref[hd, :, LANE + ROPE_DIM:QK_DIM] = zeros
        v_ref[hd, :, 0:V_DIM] = kv[:, HW:2 * HW][:, cols].astype(BF16)
        v_ref[hd, :, V_DIM:VO_DIM] = ones


def _ctx_kv(cache_ckv, cache_krope, w, l):
    bsz, _, past, _ = cache_ckv.shape
    return pl.pallas_call(
        _ctx_kv_kernel,
        out_shape=[jax.ShapeDtypeStruct((N_HEADS, bsz * past, QK_DIM), BF16),
                   jax.ShapeDtypeStruct((N_HEADS, bsz * past, VO_DIM), BF16)],
        grid=(bsz,),
        in_specs=[pl.BlockSpec((None, None, past, KV_RANK), lambda b: (b, l, 0, 0)),
                  pl.BlockSpec((None, None, past, ROPE_DIM), lambda b: (b, l, 0, 0)),
                  _layer_spec((KV_RANK, 2 * HW), l)],
        out_specs=[pl.BlockSpec((N_HEADS, past, QK_DIM), lambda b: (0, b, 0)),
                   pl.BlockSpec((N_HEADS, past, VO_DIM), lambda b: (0, b, 0))],
        compiler_params=_params("parallel"),
        name="ctx_kv",
    )(cache_ckv, cache_krope, w["w_ukv"])


def _attn_kernel(*refs, heads, has_ctx, kc, n_seq):
    if has_ctx:
        q_ref, kl_ref, vl_ref, kx_ref, vx_ref, o_ref = refs
    else:
        q_ref, kl_ref, vl_ref, o_ref = refs
    tq = q_ref.shape[1] // n_seq
    n_keys = kl_ref.shape[1] // n_seq
    for sq in range(n_seq):
        q_rows = slice(sq * tq, (sq + 1) * tq)
        for hd in range(heads):
            q = q_ref[hd, q_rows, :]
            chunks = []
            if has_ctx:
                chunks.append((kx_ref, vx_ref, 0, kx_ref.shape[1]))
            for s0 in range(0, n_keys, kc):
                chunks.append((kl_ref, vl_ref, sq * n_keys + s0, min(kc, n_keys - s0)))
            m = pv = None
            for (k_r, v_r, s0, n) in chunks:
                k = k_r[hd, s0:s0 + n, :]
                v = v_r[hd, s0:s0 + n, :]
                s = lax.dot_general(q, k, (((1,), (1,)), ((), ())), preferred_element_type=F32)
                m_c = jnp.max(s, axis=-1, keepdims=True)
                if m is None:
                    m = m_c
                    pv = _dot(jnp.exp2(s - m).astype(BF16), v)
                else:
                    m_new = jnp.maximum(m, m_c)
                    pv = jnp.exp2(m - m_new) * pv + _dot(jnp.exp2(s - m_new).astype(BF16), v)
                    m = m_new
            o_ref[q_rows, hd * V_DIM:(hd + 1) * V_DIM] = (pv[:, :V_DIM] / pv[:, V_DIM:]).astype(o_ref.dtype)


def _attention(q, k, v, ctx, *, batch, seq_len, tq, kc, heads_per_step, n_seq=1):
    t = q.shape[1]
    nq = seq_len // tq
    hb = heads_per_step
    has_ctx = ctx is not None
    assert n_seq == 1 or (nq == 1 and not has_ctx and batch % n_seq == 0)
    batch //= n_seq
    tq *= n_seq
    seq_len *= n_seq
    in_specs = [
        pl.BlockSpec((hb, tq, QK_DIM), lambda b, h, i: (h, b * nq + i, 0)),
        pl.BlockSpec((hb, seq_len, QK_DIM), lambda b, h, i: (h, b, 0)),
        pl.BlockSpec((hb, seq_len, VO_DIM), lambda b, h, i: (h, b, 0)),
    ]
    args = [q, k, v]
    if has_ctx:
        past = ctx[0].shape[1] // batch
        in_specs += [pl.BlockSpec((hb, past, QK_DIM), lambda b, h, i: (h, b, 0)),
                     pl.BlockSpec((hb, past, VO_DIM), lambda b, h, i: (h, b, 0))]
        args += list(ctx)
    return pl.pallas_call(
        functools.partial(_attn_kernel, heads=hb, has_ctx=has_ctx, kc=kc, n_seq=n_seq),
        out_shape=jax.ShapeDtypeStruct((t, N_HEADS * V_DIM), BF16),
        grid=(batch, N_HEADS // hb, nq),
        in_specs=in_specs,
        out_specs=pl.BlockSpec((tq, hb * V_DIM), lambda b, h, i: (b * nq + i, h)),
        compiler_params=_params("parallel", "parallel", "arbitrary"),
        name="attention",
    )(*args)


CONV_ROWS = 64


def _conv_copy_rows(seg):
    return seg + (HALO + CONV_K // 2) // SUBLANE * SUBLANE


def _out_proj_kernel(*refs, tm, seg, halo, tiles_per_seq):
    x_ref, mod_ref, a_ref = refs[:3]
    refs = refs[3:]
    if halo:
        ap_ref, an_ref = refs[:2]
        refs = refs[2:]
    b_ref, o_ref, cw_ref, cb_ref, lg_ref, lb_ref, wout_ref, y_ref, abuf, sh, aout, cat = refs

    cw = cw_ref[...]
    ext = _conv_copy_rows(seg)
    for sgm in range(tm // seg):
        r0 = sgm * seg
        if halo:
            ti = pl.program_id(0) % tiles_per_seq
            abuf[0:HALO, :] = ap_ref[...].astype(F32) * jnp.where(ti > 0, 1.0, 0.0)
            abuf[HALO + seg:, :] = an_ref[...].astype(F32) * jnp.where(ti < tiles_per_seq - 1, 1.0, 0.0)
        else:
            abuf[0:HALO, :] = jnp.zeros((HALO, D_A), F32)
            abuf[HALO + seg:, :] = jnp.zeros((HALO, D_A), F32)
        abuf[HALO:HALO + seg, :] = a_ref[r0:r0 + seg, :].astype(F32)
        for r in range(1, SUBLANE):
            sh[r - 1, :, :] = abuf[r:r + ext, :]
        for t0 in range(0, seg, CONV_ROWS):
            acc = jnp.zeros((CONV_ROWS, D_A), F32) + cb_ref[...]
            for kk in range(CONV_K):
                q8, r = divmod(HALO - CONV_K // 2 + kk, SUBLANE)
                rows = slice(t0 + q8 * SUBLANE, t0 + q8 * SUBLANE + CONV_ROWS)
                src = abuf[rows, :] if r == 0 else sh[r - 1, rows, :]
                acc = acc + src * cw[kk:kk + 1, :]
            an = _group_ln(acc, lg_ref[...], lb_ref[...], G_A)
            aout[r0 + t0:r0 + t0 + CONV_ROWS, :] = (an * _sigmoid(an)).astype(BF16)

    gate = mod_ref[0][2:3]
    cat[:, 0:D_B] = b_ref[...]
    cat[:, D_B:] = o_ref[...]
    y_ref[...] = x_ref[...] + gate * _dot(cat[...], wout_ref[D_A:, :])
    y_ref[...] += gate * _dot(aout[...], wout_ref[0:D_A, :])


def _out_proj(x, mod, a, b, o, w, l, *, seq_len, tm):
    t = x.shape[0]
    halo = tm < seq_len
    assert (seq_len % tm == 0) if halo else (tm % seq_len == 0)
    seg = tm if halo else seq_len
    tiles_per_seq = max(seq_len // tm, 1)
    hb = tm // HALO
    n_hb = t // HALO
    n_mod = mod.shape[0]

    def mod_idx(i):
        return (jnp.minimum(i * tm // seq_len, n_mod - 1), 0, 0)

    d_mix = D_A + D_B + N_HEADS * V_DIM
    in_specs = [
        pl.BlockSpec((tm, D_MODEL), lambda i: (i, 0)),
        pl.BlockSpec((1, 6, D_MODEL), mod_idx),
        pl.BlockSpec((tm, D_A), lambda i: (i, 0)),
    ]
    args = [x, mod, a]
    if halo:
        in_specs += [pl.BlockSpec((HALO, D_A), lambda i: (jnp.maximum(i * hb - 1, 0), 0)),
                     pl.BlockSpec((HALO, D_A), lambda i: (jnp.minimum((i + 1) * hb, n_hb - 1), 0))]
        args += [a, a]
    in_specs += [
        pl.BlockSpec((tm, D_B), lambda i: (i, 0)),
        pl.BlockSpec((tm, N_HEADS * V_DIM), lambda i: (i, 0)),
        _layer_spec((CONV_K, D_A), l), _layer_spec((1, D_A), l), _layer_spec((1, D_A), l),
        _layer_spec((1, D_A), l),
        _layer_spec((d_mix, D_MODEL), l),
    ]
    args += [b, o, w["conv_w"], w["conv_b"], w["conv_ln_g"], w["conv_ln_b"], w["w_out"]]
    return pl.pallas_call(
        functools.partial(_out_proj_kernel, tm=tm, seg=seg, halo=halo, tiles_per_seq=tiles_per_seq),
        out_shape=jax.ShapeDtypeStruct((t, D_MODEL), F32),
        grid=(t // tm,),
        in_specs=in_specs,
        out_specs=pl.BlockSpec((tm, D_MODEL), lambda i: (i, 0)),
        scratch_shapes=[pltpu.VMEM((seg + 2 * HALO, D_A), F32),
                        pltpu.VMEM((SUBLANE - 1, _conv_copy_rows(seg), D_A), F32),
                        pltpu.VMEM((tm, D_A), BF16),
                        pltpu.VMEM((tm, d_mix - D_A), BF16)],
        compiler_params=_params("parallel"),
        name="out_proj",
    )(*args)


def _ffn_kernel(*refs, tm, seq_len, halo, final_norm):
    x_ref = refs[0]
    refs = refs[1:]
    if halo:
        xp_ref, xn_ref = refs[:2]
        refs = refs[2:]
    (mod_ref, g2_ref, wg_ref, wv_ref, cwg_ref, cwv_ref, cbg_ref, cbv_ref, wdn_ref, fg_ref,
     y_ref, hs, zg, zv) = refs
    i = pl.program_id(0)
    j = pl.program_id(1)
    nj = pl.num_programs(1)
    mod = mod_ref[0]
    pad = HALO if halo else SUBLANE
    tf = zg.shape[1]

    def prologue():
        g2 = g2_ref[...]
        scale = 1.0 + mod[4:5]
        shift = mod[3:4]
        x = x_ref[...]
        if halo:
            tiles_per_seq = seq_len // tm
            ti = i % tiles_per_seq
            prev_ok = jnp.where(ti > 0, 1.0, 0.0)
            next_ok = jnp.where(ti < tiles_per_seq - 1, 1.0, 0.0)
            hs[0:HALO, :] = ((_rmsnorm(xp_ref[...], g2) * scale + shift) * prev_ok).astype(BF16)
            hs[HALO:HALO + tm, :] = (_rmsnorm(x, g2) * scale + shift).astype(BF16)
            hs[HALO + tm:, :] = ((_rmsnorm(xn_ref[...], g2) * scale + shift) * next_ok).astype(BF16)
        else:
            hs[...] = (_rmsnorm(x, g2) * scale + shift).astype(BF16)
            zeros = jnp.zeros((pad, tf), F32)
            for z in (zg, zv):
                z[0:pad, :] = zeros
                z[pad + tm:, :] = zeros

    def ff_tile():
        if halo:
            zg[...] = _dot(hs[...], wg_ref[...])
            zv[...] = _dot(hs[...], wv_ref[...])
        else:
            zg[pad:pad + tm, :] = _dot(hs[...], wg_ref[...])
            zv[pad:pad + tm, :] = _dot(hs[...], wv_ref[...])
            pos = lax.rem(lax.broadcasted_iota(jnp.int32, (tm, 1), 0), seq_len)
            not_first = pos != 0
            not_last = pos != seq_len - 1

        def conv(z, cw_ref, cb_ref):
            cw = cw_ref[...]
            zp = z[pad - 1:pad - 1 + tm, :]
            zn = z[pad + 1:pad + 1 + tm, :]
            if not halo:
                zp = jnp.where(not_first, zp, 0.0)
                zn = jnp.where(not_last, zn, 0.0)
            return cb_ref[...] + zp * cw[0:1, :] + z[pad:pad + tm, :] * cw[1:2, :] + zn * cw[2:3, :]

        g = conv(zg, cwg_ref, cbg_ref)
        act = (g * _sigmoid(g) * conv(zv, cwv_ref, cbv_ref)).astype(BF16)
        return _dot(act, wdn_ref[...])

    @pl.when(j == 0)
    def _():
        prologue()
        y_ref[...] = ff_tile()

    @pl.when((j > 0) & (j < nj - 1))
    def _():
        y_ref[...] += ff_tile()

    @pl.when(j == nj - 1)
    def _():
        y = x_ref[...] + mod[5:6] * (y_ref[...] + ff_tile())
        if final_norm:
            y = _rmsnorm(y, fg_ref[...])
        y_ref[...] = y


def _ffn(x, mod, w, l, final_g, *, seq_len, tm, tf, final_norm):
    t = x.shape[0]
    halo = tm < seq_len
    assert (seq_len % tm == 0) if halo else (tm % seq_len == 0)
    nf = D_FF // tf
    assert nf >= 2
    hb = tm // HALO
    n_hb = t // HALO
    n_mod = mod.shape[0]
    pad = HALO if halo else SUBLANE

    def mod_idx(i, j):
        return (jnp.minimum(i * tm // seq_len, n_mod - 1), 0, 0)

    def lspec(shape, idx):
        return pl.BlockSpec((None,) + shape, lambda i, j: (l,) + idx(i, j))

    in_specs = [pl.BlockSpec((tm, D_MODEL), lambda i, j: (i, 0))]
    args = [x]
    if halo:
        in_specs += [
            pl.BlockSpec((HALO, D_MODEL), lambda i, j: (jnp.maximum(i * hb - 1, 0), 0)),
            pl.BlockSpec((HALO, D_MODEL), lambda i, j: (jnp.minimum((i + 1) * hb, n_hb - 1), 0)),
        ]
        args += [x, x]
    in_specs += [
        pl.BlockSpec((1, 6, D_MODEL), mod_idx),
        _layer_spec((1, D_MODEL), l),
        lspec((D_MODEL, tf), lambda i, j: (0, j)),
        lspec((D_MODEL, tf), lambda i, j: (0, nf + j)),
        lspec((FFN_K, tf), lambda i, j: (0, j)),
        lspec((FFN_K, tf), lambda i, j: (0, nf + j)),
        lspec((1, tf), lambda i, j: (0, j)),
        lspec((1, tf), lambda i, j: (0, nf + j)),
        lspec((tf, D_MODEL), lambda i, j: (j, 0)),
        pl.BlockSpec((1, D_MODEL), lambda i, j: (0, 0)),
    ]
    args += [mod, w["norm2_g"], w["w_up"], w["w_up"], w["ffn_conv_w"], w["ffn_conv_w"],
             w["ffn_conv_b"], w["ffn_conv_b"], w["w_down"], final_g]
    h_rows = tm + 2 * HALO if halo else tm
    return pl.pallas_call(
        functools.partial(_ffn_kernel, tm=tm, seq_len=seq_len, halo=halo, final_norm=final_norm),
        out_shape=jax.ShapeDtypeStruct((t, D_MODEL), F32),
        grid=(t // tm, nf),
        in_specs=in_specs,
        out_specs=pl.BlockSpec((tm, D_MODEL), lambda i, j: (i, 0)),
        scratch_shapes=[pltpu.VMEM((h_rows, D_MODEL), BF16),
                        pltpu.VMEM((tm + 2 * pad, tf), F32),
                        pltpu.VMEM((tm + 2 * pad, tf), F32)],
        compiler_params=_params("parallel", "arbitrary"),
        name="conv_ffn",
    )(*args)


def _rope_tables(length):
    pos = np.arange(length)
    r = (pos // GRID_W).astype(np.float64)
    col = (pos % GRID_W).astype(np.float64)
    half = ROPE_DIM // 4
    freq = 1.0 / (ROPE_THETA ** (np.arange(half, dtype=np.float64) / half))
    ar = r[:, None] * freq
    ac = col[:, None] * freq
    zeros = np.zeros((length, LANE - ROPE_DIM))
    cos = np.concatenate([np.cos(ar), np.cos(ar), np.cos(ac), np.cos(ac), zeros], axis=1)
    sin = np.concatenate([-np.sin(ar), np.sin(ar), -np.sin(ac), np.sin(ac), zeros], axis=1)
    return jnp.asarray(cos, F32), jnp.asarray(sin, F32)


def _swap_rope_cols(w):
    q = ROPE_DIM // 4
    return jnp.concatenate([w[..., q:2 * q], w[..., 0:q], w[..., 3 * q:4 * q], w[..., 2 * q:3 * q]], axis=-1)


def _pack_w_in_kernel(wt_ref, sel_ref, o_ref):
    rows = wt_ref.shape[1]
    wt = wt_ref[...].astype(BF16)
    eye = (lax.broadcasted_iota(jnp.int32, (rows, rows), 0)
           == lax.broadcasted_iota(jnp.int32, (rows, rows), 1)).astype(BF16)
    nt = (((1,), (1,)), ((), ()))
    o_ref[:, 0:C_ZKR] = lax.dot_general(eye, wt[0:C_ZKR], nt, preferred_element_type=F32).astype(BF16)
    zkr_w = lax.dot_general(eye, wt[C_ZKR:C_ZKR + ROPE_DIM], nt, preferred_element_type=F32).astype(BF16)
    o_ref[:, C_ZKR:] = _dot(zkr_w, sel_ref[...]).astype(BF16)


def _pack_w_in(w_in):
    q = ROPE_DIM // 4
    partner = np.concatenate([np.arange(q, 2 * q), np.arange(0, q), np.arange(3 * q, 4 * q), np.arange(2 * q, 3 * q)])
    sel = np.zeros((ROPE_DIM, 2 * LANE), np.float32)
    sel[np.arange(ROPE_DIM), np.arange(ROPE_DIM)] = 1.0
    sel[partner, LANE + np.arange(ROPE_DIM)] = 1.0
    rows = 2 * LANE
    n_in = w_in.shape[-1]
    return pl.pallas_call(
        _pack_w_in_kernel,
        out_shape=jax.ShapeDtypeStruct((DEPTH, D_MODEL, N_IN_EXT), BF16),
        grid=(DEPTH, D_MODEL // rows),
        in_specs=[pl.BlockSpec((None, n_in, rows), lambda l, i: (l, 0, i)),
                  pl.BlockSpec((ROPE_DIM, 2 * LANE), lambda l, i: (0, 0))],
        out_specs=pl.BlockSpec((None, rows, N_IN_EXT), lambda l, i: (l, i, 0)),
        compiler_params=_params("parallel", "parallel"),
        name="pack_w_in",
    )(jnp.swapaxes(w_in, 1, 2), jnp.asarray(sel, BF16))


def _prep_weights(p):

    wuq = p["w_uq"].reshape(DEPTH, Q_RANK, N_HEADS, NOPE_DIM + ROPE_DIM)
    nope = wuq[..., :NOPE_DIM].reshape(DEPTH, Q_RANK, HW)
    rope = wuq[..., NOPE_DIM:]
    rope_p = jnp.concatenate([rope, _swap_rope_cols(rope)], axis=-1).reshape(DEPTH, Q_RANK, HW)

    wukv = p["w_ukv"].reshape(DEPTH, KV_RANK, N_HEADS, NOPE_DIM + V_DIM)
    wukv = jnp.concatenate([wukv[..., :NOPE_DIM].reshape(DEPTH, KV_RANK, HW),
                            wukv[..., NOPE_DIM:].reshape(DEPTH, KV_RANK, HW)], axis=-1)

    def vec(name):
        return p[name][:, None, :]

    return {
        "norm1_g": vec("norm1_g"),
        "w_in": _pack_w_in(p["w_in"]),
        "sg_ln_g": vec("sg_ln_g"), "sg_ln_b": vec("sg_ln_b"),
        "sg_w": p["sg_w"].astype(BF16),
        "sg_bias": jnp.broadcast_to(p["sg_b"][..., None], (DEPTH, G_B, CHUNK, LANE)),
        "q_norm_g": vec("q_norm_g"),
        "w_uq": jnp.concatenate([nope, rope_p], axis=-1).astype(BF16),
        "kv_norm_g": vec("kv_norm_g"),
        "w_ukv": wukv.astype(BF16),
        "conv_w": p["conv_w"], "conv_b": vec("conv_b"),
        "conv_ln_g": vec("conv_ln_g"), "conv_ln_b": vec("conv_ln_b"),
        "w_out": p["w_out"].astype(BF16),
        "norm2_g": vec("norm2_g"),
        "w_up": p["ffn_w_up"].astype(BF16),
        "ffn_conv_w": p["ffn_conv_w"], "ffn_conv_b": vec("ffn_conv_b"),
        "w_down": p["ffn_w_down"].astype(BF16),
    }


TM_IN = 512
TM_OUT = 512
TM_FFN = 1024
TF_FFN = 512
TQ_LAT = 1024
KC_LAT = 1024
SEQS_CTX = 4


def kernel(x_prompt, x_sample, cache_ckv, cache_krope, c, c_ctx, norm1_g, w_ada, b_ada, w_in, conv_w, conv_b, conv_ln_g, conv_ln_b, sg_ln_g, sg_ln_b, sg_w, sg_b, q_norm_g, w_uq, kv_norm_g, w_ukv, w_out, norm2_g, ffn_w_up, ffn_conv_w, ffn_conv_b, ffn_w_down, final_g):
    w = _prep_weights(dict(
        norm1_g=norm1_g, w_in=w_in, conv_w=conv_w, conv_b=conv_b, conv_ln_g=conv_ln_g,
        conv_ln_b=conv_ln_b, sg_ln_g=sg_ln_g, sg_ln_b=sg_ln_b, sg_w=sg_w, sg_b=sg_b,
        q_norm_g=q_norm_g, w_uq=w_uq, kv_norm_g=kv_norm_g, w_ukv=w_ukv, w_out=w_out,
        norm2_g=norm2_g, ffn_w_up=ffn_w_up, ffn_conv_w=ffn_conv_w, ffn_conv_b=ffn_conv_b,
        ffn_w_down=ffn_w_down))
    bp, lp_, _ = x_prompt.shape
    bs, ls, _ = x_sample.shape

    cvec = jnp.concatenate([c_ctx[None], c, jnp.zeros((8 - 1 - bs, D_MODEL), F32)], axis=0)
    mod_all = _ada_mod(cvec, w_ada, b_ada).reshape(DEPTH, 8, 6, D_MODEL)
    rope = _rope_tables(ls)
    fg = final_g[None]

    xp = x_prompt.reshape(bp * lp_, D_MODEL)
    xs = x_sample.reshape(bs * ls, D_MODEL)
    ckv_out, kr_out = [], []
    for l in range(DEPTH):
        last = l == DEPTH - 1
        mod_ctx = mod_all[l, 0:1]
        mod_lat = mod_all[l, 1:1 + bs]

        a, b, q, k, v, ckv_n, kr = _in_proj(xp, mod_ctx, w, l, seq_len=lp_, rope=None, emit_cache=True, tm=TM_IN)
        ckv_out.append(ckv_n.reshape(bp, lp_, KV_RANK))
        kr_out.append(kr.reshape(bp, lp_, ROPE_DIM))
        o = _attention(q, k, v, None, batch=bp, seq_len=lp_, tq=lp_, kc=lp_, heads_per_step=N_HEADS,
                       n_seq=SEQS_CTX)
        xp = _out_proj(xp, mod_ctx, a, b, o, w, l, seq_len=lp_, tm=TM_OUT)
        xp = _ffn(xp, mod_ctx, w, l, fg, seq_len=lp_, tm=TM_FFN, tf=TF_FFN, final_norm=last)

        ctx = _ctx_kv(cache_ckv, cache_krope, w, l)
        a, b, q, k, v = _in_proj(xs, mod_lat, w, l, seq_len=ls, rope=rope, emit_cache=False, tm=TM_IN)
        o = _attention(q, k, v, ctx, batch=bs, seq_len=ls, tq=TQ_LAT, kc=KC_LAT, heads_per_step=4)
        xs = _out_proj(xs, mod_lat, a, b, o, w, l, seq_len=ls, tm=TM_OUT)
        xs = _ffn(xs, mod_lat, w, l, fg, seq_len=ls, tm=TM_FFN, tf=TF_FFN, final_norm=last)

    return (xp.reshape(bp, lp_, D_MODEL), xs.reshape(bs, ls, D_MODEL),
            jnp.stack(ckv_out, axis=1), jnp.stack(kr_out, axis=1))
```
